```python
import jax, jax.numpy as jnp
from jax import lax
import numpy as np

D_MODEL = 2048
BATCH = 2
SEQ = 4096
DEPTH = 4

N_MEM = 256
HEAD_DIM = 128
N_MEM_HEADS = 4
MEM_W = N_MEM_HEADS * HEAD_DIM
MIX_W = D_MODEL - MEM_W
DN_HEADS = MIX_W // HEAD_DIM
DN_CHUNK = 64
CONV_W = 5
GM_GROUPS = MIX_W // HEAD_DIM
GM_CHUNK = 128
FFN_HIDDEN = 4 * D_MODEL
N_MIXERS = 2
N_DN_LAYERS = (DEPTH + 1) // 2
N_GM_LAYERS = DEPTH // 2
DN_IN = 4 * MIX_W + 4 * DN_HEADS + MEM_W
GM_IN = 2 * MIX_W + MEM_W
RMS_EPS = 1e-6
L2_EPS = 1e-6

kernel_name = "bidir_hybrid_deltanet_gmlp_memxattn"


def rmsnorm(x, g):
    x32 = x.astype(jnp.float32)
    y = x32 * lax.rsqrt(jnp.mean(jnp.square(x32), axis=-1, keepdims=True) + RMS_EPS)
    return (y * g.astype(jnp.float32)).astype(x.dtype)


def l2norm(x):
    x32 = x.astype(jnp.float32)
    return x32 * lax.rsqrt(jnp.sum(jnp.square(x32), axis=-1, keepdims=True) + L2_EPS)


def centred_depthwise_conv(x, w):
    c = x.shape[-1]
    return lax.conv_general_dilated(
        x, w[:, None, :].astype(x.dtype), window_strides=(1,),
        padding=[(CONV_W // 2, CONV_W // 2)],
        dimension_numbers=("NWC", "WIO", "NWC"), feature_group_count=c)


def chunk_gated_delta(q, k, v, g, beta):
    b_, s_, h_, dk = q.shape
    dv = v.shape[-1]
    n_ch, c = s_ // DN_CHUNK, DN_CHUNK

    def blocks(t):
        return jnp.moveaxis(t.reshape((b_, n_ch, c, h_) + t.shape[3:]), 3, 1)

    q, k, v, g, beta = blocks(q), blocks(k), blocks(v), blocks(g), blocks(beta)
    gc = jnp.cumsum(g, axis=-1)
    incl = jnp.tril(jnp.ones((c, c), dtype=bool))
    strict = jnp.tril(jnp.ones((c, c), dtype=bool), -1)
    diff = gc[..., :, None] - gc[..., None, :]
    decay = jnp.exp(jnp.where(incl, diff, -jnp.inf))
    kb = k * beta[..., None]
    a_low = jnp.where(strict, jnp.einsum("bhnid,bhnjd->bhnij", kb, k) * decay, 0.0)
    lower = a_low + jnp.eye(c, dtype=a_low.dtype)
    rhs = jnp.concatenate([v * beta[..., None], kb * jnp.exp(gc)[..., None]], axis=-1)
    sol = lax.linalg.triangular_solve(lower, rhs, left_side=True, lower=True,
                                      unit_diagonal=True)
    u, w = sol[..., :dv], sol[..., dv:]
    attn = jnp.einsum("bhnid,bhnjd->bhnij", q, k) * decay
    q_dec = q * jnp.exp(gc)[..., None]
    g_last = gc[..., -1:]
    k_dec = k * jnp.exp(g_last - gc)[..., None]
    chunk_decay = jnp.exp(g_last)[..., None]
    xs = tuple(jnp.moveaxis(t, 2, 0) for t in (u, w, q_dec, k_dec, attn, chunk_decay))

    def step(state, inp):
        u_n, w_n, qd_n, kd_n, at_n, cd_n = inp
        v_new = u_n - jnp.einsum("bhck,bhkv->bhcv", w_n, state)
        o = (jnp.einsum("bhck,bhkv->bhcv", qd_n, state)
             + jnp.einsum("bhij,bhjv->bhiv", at_n, v_new))
        state = state * cd_n + jnp.einsum("bhck,bhcv->bhkv", kd_n, v_new)
        return state, o

    state0 = jnp.zeros((b_, h_, dk, dv), jnp.float32)
    _, o = lax.scan(step, state0, xs)
    return jnp.transpose(o, (1, 0, 3, 2, 4)).reshape(b_, s_, h_, dv)


def deltanet_mixer(xn, w_in, conv_w, a_log, dt_bias, out_g):
    b_, s_, _ = xn.shape
    h_ = DN_HEADS
    proj = xn @ w_in
    qkv, z, a, bt, q_mem = jnp.split(
        proj, [3 * MIX_W, 4 * MIX_W, 4 * MIX_W + 2 * h_, 4 * MIX_W + 4 * h_], axis=-1)
    qkv = jax.nn.silu(centred_depthwise_conv(qkv, conv_w))
    q, k, v = jnp.split(qkv, 3, axis=-1)
    q = l2norm(q.reshape(b_, s_, h_, HEAD_DIM)) * (HEAD_DIM ** -0.5)
    k = l2norm(k.reshape(b_, s_, h_, HEAD_DIM))
    v = v.reshape(b_, s_, h_, HEAD_DIM).astype(jnp.float32)
    g = -jnp.exp(a_log.reshape(-1).astype(jnp.float32)) * jax.nn.softplus(
        a.astype(jnp.float32) + dt_bias.reshape(-1).astype(jnp.float32))
    beta = jax.nn.sigmoid(bt.astype(jnp.float32))
    o_f = chunk_gated_delta(q, k, v, g[..., :h_], beta[..., :h_])
    flip = lambda t: jnp.flip(t, axis=1)
    o_b = flip(chunk_gated_delta(flip(q), flip(k), flip(v),
                                 flip(g[..., h_:]), flip(beta[..., h_:])))
    o = rmsnorm(o_f + o_b, out_g) * jax.nn.silu(
        z.reshape(b_, s_, h_, HEAD_DIM).astype(jnp.float32))
    return o.reshape(b_, s_, MIX_W).astype(xn.dtype), q_mem


def spatial_gating_mixer(xn, w_in, v_g, w_s, b_s):
    b_, s_, _ = xn.shape
    proj = xn @ w_in
    u, v, q_mem = jnp.split(proj, [MIX_W, 2 * MIX_W], axis=-1)
    u = jax.nn.gelu(u)
    v = rmsnorm(jax.nn.gelu(v), v_g)
    v = v.reshape(b_, s_ // GM_CHUNK, GM_CHUNK, GM_GROUPS, HEAD_DIM)
    sg = (jnp.einsum("gij,bnjge->bnige", w_s.astype(v.dtype), v)
          + b_s.T[None, None, :, :, None].astype(v.dtype))
    return u * sg.reshape(b_, s_, MIX_W), q_mem


def memory_attention(q_mem, mem_n, w_kv):
    b_, s_, _ = q_mem.shape
    kv = mem_n @ w_kv
    k, v = jnp.split(kv, 2, axis=-1)
    k = k.reshape(b_, -1, N_MEM_HEADS, HEAD_DIM).astype(jnp.float32)
    v = v.reshape(b_, -1, N_MEM_HEADS, HEAD_DIM).astype(jnp.float32)
    q = q_mem.reshape(b_, s_, N_MEM_HEADS, HEAD_DIM).astype(jnp.float32)
    scores = jnp.einsum("bshd,bmhd->bhsm", q, k) * (HEAD_DIM ** -0.5)
    p = jax.nn.softmax(scores, axis=-1)
    o = jnp.einsum("bhsm,bmhd->bshd", p, v)
    return o.reshape(b_, s_, MEM_W).astype(q_mem.dtype)


def setup_inputs(seed: int = 0) -> dict:
    key = jax.random.key(seed)
    ks = jax.random.split(key, 24)
    f32 = jnp.float32
    nrm = lambda k, shape, scale: jax.random.normal(k, shape, f32) * scale
    gain = lambda k, shape: 1.0 + 0.05 * jax.random.normal(k, shape, f32)
    dt = jnp.exp(jax.random.uniform(ks[5], (N_DN_LAYERS, 2, DN_HEADS), f32)
                 * (np.log(0.1) - np.log(0.001)) + np.log(0.001))
    return {
        "x": nrm(ks[0], (BATCH, SEQ, D_MODEL), 1.0),
        "mem": nrm(ks[1], (BATCH, N_MEM, D_MODEL), 1.0),
        "mem_norm_g": gain(ks[2], (D_MODEL,)),
        "tok_norm_g": gain(ks[3], (DEPTH, D_MODEL)),
        "dn_w_in": nrm(ks[4], (N_DN_LAYERS, D_MODEL, DN_IN), D_MODEL ** -0.5),
        "dn_conv_w": nrm(ks[6], (N_DN_LAYERS, CONV_W, 3 * MIX_W), CONV_W ** -0.5),
        "dn_a_log": jnp.log(jax.random.uniform(ks[7], (N_DN_LAYERS, 2, DN_HEADS), f32, 1.0, 16.0)),
        "dn_dt_bias": dt + jnp.log(-jnp.expm1(-dt)),
        "dn_out_norm_g": gain(ks[8], (N_DN_LAYERS, HEAD_DIM)),
        "gm_w_in": nrm(ks[9], (N_GM_LAYERS, D_MODEL, GM_IN), D_MODEL ** -0.5),
        "gm_v_norm_g": gain(ks[10], (N_GM_LAYERS, MIX_W)),
        "gm_w_s": nrm(ks[11], (N_GM_LAYERS, GM_GROUPS, GM_CHUNK, GM_CHUNK), 0.5 * GM_CHUNK ** -0.5),
        "gm_b_s": gain(ks[12], (N_GM_LAYERS, GM_GROUPS, GM_CHUNK)),
        "mem_w_kv": nrm(ks[13], (DEPTH, D_MODEL, 2 * MEM_W), D_MODEL ** -0.5),
        "tok_w_out": nrm(ks[14], (DEPTH, MIX_W + MEM_W, D_MODEL), (MIX_W + MEM_W) ** -0.5),
        "ffn_norm_g": gain(ks[15], (DEPTH, D_MODEL)),
        "ffn_w1": nrm(ks[16], (DEPTH, D_MODEL, FFN_HIDDEN), D_MODEL ** -0.5),
        "ffn_w2": nrm(ks[17], (DEPTH, FFN_HIDDEN, D_MODEL), FFN_HIDDEN ** -0.5),
        "final_norm_g": gain(ks[18], (D_MODEL,)),
    }


def reference(x, mem, mem_norm_g, tok_norm_g, dn_w_in, dn_conv_w, dn_a_log, dn_dt_bias,
              dn_out_norm_g, gm_w_in, gm_v_norm_g, gm_w_s, gm_b_s, mem_w_kv, tok_w_out,
              ffn_norm_g, ffn_w1, ffn_w2, final_norm_g):
    mem_n = rmsnorm(mem, mem_norm_g)
    h = x
    for layer in range(DEPTH):
        j = layer // N_MIXERS
        xn = rmsnorm(h, tok_norm_g[layer])
        if layer % N_MIXERS == 0:
            mix, q_mem = deltanet_mixer(xn, dn_w_in[j], dn_conv_w[j], dn_a_log[j],
                                        dn_dt_bias[j], dn_out_norm_g[j])
        else:
            mix, q_mem = spatial_gating_mixer(xn, gm_w_in[j], gm_v_norm_g[j],
                                              gm_w_s[j], gm_b_s[j])
        mo = memory_attention(q_mem, mem_n, mem_w_kv[layer])
        h = h + jnp.concatenate([mix, mo], axis=-1) @ tok_w_out[layer]
        hn = rmsnorm(h, ffn_norm_g[layer])
        h = h + jnp.square(jax.nn.relu(hn @ ffn_w1[layer])) @ ffn_w2[layer]
    return rmsnorm(h, final_norm_g)
```

```python
import functools

import jax
import jax.numpy as jnp
from jax import lax
from jax.experimental import pallas as pl
from jax.experimental.pallas import tpu as pltpu

D_MODEL = 2048
DEPTH = 4
N_MEM = 256
HEAD_DIM = 128
N_MEM_HEADS = 4
MEM_W = N_MEM_HEADS * HEAD_DIM
MIX_W = D_MODEL - MEM_W
DN_HEADS = MIX_W // HEAD_DIM
DN_CHUNK = 64
CONV_W = 5
GM_GROUPS = MIX_W // HEAD_DIM
GM_CHUNK = 128
FFN_HIDDEN = 4 * D_MODEL
RMS_EPS = 1e-6
L2_EPS = 1e-6

F32 = jnp.float32
BF16 = jnp.bfloat16

DN_ROWS = 256
CONV_HALO = 16
NORM_ROWS = 256
VMEM_LIMIT = 52 * 1024 * 1024


def _cparams(semantics):
    return pltpu.CompilerParams(dimension_semantics=semantics, vmem_limit_bytes=VMEM_LIMIT)


def _rmsnorm_rows(x, g):
    ms = jnp.mean(x * x, axis=-1, keepdims=True)
    return x * lax.rsqrt(ms + RMS_EPS) * g


def _norm_matmul_kernel(x_ref, g_ref, w_ref, o_ref, xn_ref, *, rows):
    @pl.when(pl.program_id(1) == 0)
    def _():
        def body(r, c):
            sl = pl.ds(pl.multiple_of(r * rows, rows), rows)
            xn_ref[sl, :] = _rmsnorm_rows(x_ref[sl, :], g_ref[...]).astype(xn_ref.dtype)
            return c

        lax.fori_loop(0, x_ref.shape[0] // rows, body, 0)

    o_ref[...] = jnp.dot(xn_ref[...], w_ref[...], preferred_element_type=F32).astype(o_ref.dtype)


def _norm_matmul(x, g, w, out_dtype, tm, tn):
    m, k = x.shape
    n = w.shape[1]
    rows = min(NORM_ROWS, tm)
    return pl.pallas_call(
        functools.partial(_norm_matmul_kernel, rows=rows),
        grid=(m // tm, n // tn),
        in_specs=[
            pl.BlockSpec((tm, k), lambda i, j: (i, 0)),
            pl.BlockSpec((1, k), lambda i, j: (0, 0)),
            pl.BlockSpec((k, tn), lambda i, j: (0, j)),
        ],
        out_specs=pl.BlockSpec((tm, tn), lambda i, j: (i, j)),
        out_shape=jax.ShapeDtypeStruct((m, n), out_dtype),
        scratch_shapes=[pltpu.VMEM((tm, k), BF16)],
        compiler_params=_cparams(("parallel", "arbitrary")),
        name="norm_matmul",
    )(x, g.reshape(1, k), w)


def _out_proj_kernel(h_ref, mix_ref, mo_ref, wa_ref, wb_ref, o_ref):
    acc = jnp.dot(mix_ref[...], wa_ref[...], preferred_element_type=F32)
    acc = acc + jnp.dot(mo_ref[...], wb_ref[...], preferred_element_type=F32)
    o_ref[...] = h_ref[...] + acc


def _out_proj(h, mix, mo, wa, wb, tm, tn):
    m, d = h.shape
    return pl.pallas_call(
        _out_proj_kernel,
        grid=(m // tm, d // tn),
        in_specs=[
            pl.BlockSpec((tm, tn), lambda i, j: (i, j)),
            pl.BlockSpec((tm, mix.shape[1]), lambda i, j: (i, 0)),
            pl.BlockSpec((tm, mo.shape[1]), lambda i, j: (i, 0)),
            pl.BlockSpec((wa.shape[0], tn), lambda i, j: (0, j)),
            pl.BlockSpec((wb.shape[0], tn), lambda i, j: (0, j)),
        ],
        out_specs=pl.BlockSpec((tm, tn), lambda i, j: (i, j)),
        out_shape=jax.ShapeDtypeStruct((m, d), F32),
        compiler_params=_cparams(("parallel", "arbitrary")),
        name="out_proj",
    )(h, mix, mo, wa, wb)


def _ffn_kernel(h_ref, g_ref, w1_ref, w2_ref, fg_ref, o_ref, xn_ref, *, rows, final_norm):
    j = pl.program_id(1)
    n_rows = h_ref.shape[0] // rows

    @pl.when(j == 0)
    def _():
        def body(r, c):
            sl = pl.ds(pl.multiple_of(r * rows, rows), rows)
            x = h_ref[sl, :]
            xn_ref[sl, :] = _rmsnorm_rows(x, g_ref[...]).astype(xn_ref.dtype)
            o_ref[sl, :] = x
            return c

        lax.fori_loop(0, n_rows, body, 0)

    a = jnp.dot(xn_ref[...], w1_ref[...], preferred_element_type=F32)
    a = jnp.square(jnp.maximum(a, 0.0)).astype(BF16)
    o_ref[...] += jnp.dot(a, w2_ref[...], preferred_element_type=F32)

    if final_norm:
        @pl.when(j == pl.num_programs(1) - 1)
        def _():
            def body(r, c):
                sl = pl.ds(pl.multiple_of(r * rows, rows), rows)
                o_ref[sl, :] = _rmsnorm_rows(o_ref[sl, :], fg_ref[...])
                return c

            lax.fori_loop(0, n_rows, body, 0)


def _ffn(h, g, w1, w2, fg, final_norm, tm, tf):
    m, d = h.shape
    f = w1.shape[1]
    rows = min(NORM_ROWS, tm)
    return pl.pallas_call(
        functools.partial(_ffn_kernel, rows=rows, final_norm=final_norm),
        grid=(m // tm, f // tf),
        in_specs=[
            pl.BlockSpec((tm, d), lambda i, j: (i, 0)),
            pl.BlockSpec((1, d), lambda i, j: (0, 0)),
            pl.BlockSpec((d, tf), lambda i, j: (0, j)),
            pl.BlockSpec((tf, d), lambda i, j: (j, 0)),
            pl.BlockSpec((1, d), lambda i, j: (0, 0)),
        ],
        out_specs=pl.BlockSpec((tm, d), lambda i, j: (i, 0)),
        out_shape=jax.ShapeDtypeStruct((m, d), F32),
        scratch_shapes=[pltpu.VMEM((tm, d), BF16)],
        compiler_params=_cparams(("parallel", "arbitrary")),
        name="ffn",
    )(h, g.reshape(1, d), w1, w2, fg.reshape(1, d))


def _mem_attn_kernel(q_ref, kv_ref, o_ref):
    scale = HEAD_DIM ** -0.5
    for hh in range(N_MEM_HEADS):
        lo, hi = hh * HEAD_DIM, (hh + 1) * HEAD_DIM
        q = q_ref[:, lo:hi]
        k = kv_ref[:, lo:hi]
        v = kv_ref[:, MEM_W + lo:MEM_W + hi]
        s = lax.dot_general(q, k, (((1,), (1,)), ((), ())), preferred_element_type=F32) * scale
        s = s - jnp.max(s, axis=-1, keepdims=True)
        p = jnp.exp(s)
        l = jnp.sum(p, axis=-1, keepdims=True)
        o = jnp.dot(p.astype(BF16), v, preferred_element_type=F32) / l
        o_ref[:, lo:hi] = o.astype(o_ref.dtype)


def _mem_attn(proj, kv, batch, seq, q_block, tq):
    n_q = seq // tq
    return pl.pallas_call(
        _mem_attn_kernel,
        grid=(batch, n_q),
        in_specs=[
            pl.BlockSpec((tq, MEM_W), lambda b, i: (b * n_q + i, q_block)),
            pl.BlockSpec((N_MEM, 2 * MEM_W), lambda b, i: (b, 0)),
        ],
        out_specs=pl.BlockSpec((tq, MEM_W), lambda b, i: (b * n_q + i, 0)),
        out_shape=jax.ShapeDtypeStruct((batch * seq, MEM_W), BF16),
        compiler_params=_cparams(("parallel", "arbitrary")),
        name="mem_attn",
    )(proj, kv)


def _gm_kernel(u_ref, v_ref, vg_ref, ws_ref, bs_ref, o_ref, *, n_chunks):
    v = jax.nn.gelu(v_ref[...].astype(F32))
    vn = _rmsnorm_rows(v, vg_ref[...]).astype(BF16)
    for g in range(GM_GROUPS):
        lo, hi = g * HEAD_DIM, (g + 1) * HEAD_DIM
        rhs = jnp.concatenate(
            [vn[c * GM_CHUNK:(c + 1) * GM_CHUNK, lo:hi] for c in range(n_chunks)], axis=1)
        sg = jnp.dot(ws_ref[g], rhs, preferred_element_type=F32) + bs_ref[:, g:g + 1]
        for c in range(n_chunks):
            rows = slice(c * GM_CHUNK, (c + 1) * GM_CHUNK)
            u = jax.nn.gelu(u_ref[rows, lo:hi].astype(F32))
            o_ref[rows, lo:hi] = (u * sg[:, c * HEAD_DIM:(c + 1) * HEAD_DIM]).astype(o_ref.dtype)


def _gm_mixer(proj, v_g, w_s, b_s_t, ts):
    t = proj.shape[0]
    return pl.pallas_call(
        functools.partial(_gm_kernel, n_chunks=ts // GM_CHUNK),
        grid=(t // ts,),
        in_specs=[
            pl.BlockSpec((ts, MIX_W), lambda i: (i, 0)),
            pl.BlockSpec((ts, MIX_W), lambda i: (i, 1)),
            pl.BlockSpec((1, MIX_W), lambda i: (0, 0)),
            pl.BlockSpec((GM_GROUPS, GM_CHUNK, GM_CHUNK), lambda i: (0, 0, 0)),
            pl.BlockSpec((GM_CHUNK, GM_GROUPS), lambda i: (0, 0)),
        ],
        out_specs=pl.BlockSpec((ts, MIX_W), lambda i: (i, 0)),
        out_shape=jax.ShapeDtypeStruct((t, MIX_W), BF16),
        compiler_params=_cparams(("parallel",)),
        name="gm_mixer",
    )(proj, proj, v_g.reshape(1, MIX_W), w_s, b_s_t)


def _softplus(x):
    return jnp.maximum(x, 0.0) + jnp.log1p(jnp.exp(-jnp.abs(x)))


def _split3(x):
    hi = x.astype(BF16)
    r1 = x - hi.astype(F32)
    mid = r1.astype(BF16)
    lo = (r1 - mid.astype(F32)).astype(BF16)
    return hi, mid, lo


def _bdot(a, b):
    return jnp.dot(a.astype(BF16), b.astype(BF16), preferred_element_type=F32)


def _conv_silu(x_ref, cw_ref, m, n_blocks):
    r = DN_ROWS
    seq = x_ref.shape[0]
    row0 = pl.multiple_of(m * r, r)
    main = x_ref[pl.ds(row0, r), :].astype(F32)
    p0 = pl.multiple_of(jnp.maximum(row0 - CONV_HALO, 0), CONV_HALO)
    n0 = pl.multiple_of(jnp.minimum(row0 + r, seq - CONV_HALO), CONV_HALO)
    prev = jnp.where(m > 0, x_ref[pl.ds(p0, CONV_HALO), :].astype(F32), 0.0)
    nxt = jnp.where(m < n_blocks - 1, x_ref[pl.ds(n0, CONV_HALO), :].astype(F32), 0.0)
    ext = jnp.concatenate([prev, main, nxt], axis=0)
    n_ext = r + 2 * CONV_HALO
    acc = None
    for j in range(CONV_W):
        shift = (CONV_W // 2 - j) % n_ext
        shifted = ext if shift == 0 else pltpu.roll(ext, shift, 0)
        term = shifted[CONV_HALO:CONV_HALO + r, :] * cw_ref[j:j + 1, :]
        acc = term if acc is None else acc + term
    return acc * jax.nn.sigmoid(acc)


def _dn_kernel(q_ref, k_ref, v_ref, z_ref, gate_ref, cwq_ref, cwk_ref, cwv_ref, alog_ref, dtb_ref,
               ong_ref, o_ref, u_sc, wq_sc, ak_sc, cd_sc, oacc_sc, *, n_blocks):
    r = DN_ROWS
    head = pl.program_id(1)

    row_i = lax.broadcasted_iota(jnp.int32, (r, r), 0)
    col_j = lax.broadcasted_iota(jnp.int32, (r, r), 1)
    lane = lax.broadcasted_iota(jnp.int32, (1, HEAD_DIM), 1)
    sub = lax.broadcasted_iota(jnp.int32, (HEAD_DIM, 1), 0)
    tri_incl = jnp.where(col_j <= row_i, 1.0, 0.0).astype(BF16)

    def pick_col(x, idx):
        return jnp.sum(jnp.where(lane == idx, x, 0.0), axis=1, keepdims=True)

    def pick_row(xt, idx):
        return jnp.sum(jnp.where(sub == idx, xt, 0.0), axis=0, keepdims=True)

    def prep(m, carry):
        row0 = pl.multiple_of(m * r, r)
        q = _conv_silu(q_ref, cwq_ref, m, n_blocks)
        k = _conv_silu(k_ref, cwk_ref, m, n_blocks)
        v = _conv_silu(v_ref, cwv_ref, m, n_blocks)
        q = q * lax.rsqrt(jnp.sum(q * q, axis=-1, keepdims=True) + L2_EPS) * (HEAD_DIM ** -0.5)
        k = k * lax.rsqrt(jnp.sum(k * k, axis=-1, keepdims=True) + L2_EPS)
        kb = k.astype(BF16)
        qkk = lax.dot_general(jnp.concatenate([q.astype(BF16), kb], axis=0), kb,
                              (((1,), (1,)), ((), ())), preferred_element_type=F32)
        qk, kk = qkk[:r], qkk[r:]

        graw = gate_ref[pl.ds(row0, r), :]
        gval = -jnp.exp(alog_ref[...]) * _softplus(graw + dtb_ref[...])
        bval = jax.nn.sigmoid(graw)
        hi, mid, lo = _split3(gval)
        cs = jnp.dot(tri_incl, jnp.concatenate([hi, mid, lo], axis=1), preferred_element_type=F32)
        gc_f = cs[:, :HEAD_DIM] + cs[:, HEAD_DIM:2 * HEAD_DIM] + cs[:, 2 * HEAD_DIM:]
        gc_b = gc_f[r - 1:r, :] - gc_f + gval

        for d, gc_all in enumerate((gc_f, gc_b)):
            lg = d * DN_HEADS + head
            gc_col = pick_col(gc_all, lg)
            gc_row = pick_row(gc_all.T, lg)
            beta = pick_col(bval, 2 * DN_HEADS + lg)
            g_last = gc_col[r - 1:r, :] if d == 0 else gc_col[0:1, :]
            earlier = (col_j < row_i) if d == 0 else (col_j > row_i)
            dec = jnp.exp(jnp.minimum(gc_col - gc_row, 0.0))
            a_full = jnp.where(earlier, kk * beta * dec, 0.0)
            same64 = (row_i // DN_CHUNK) == (col_j // DN_CHUNK)
            same128 = (row_i // (2 * DN_CHUNK)) == (col_j // (2 * DN_CHUNK))
            a_bd = jnp.where(same64, a_full, 0.0)
            l1 = jnp.where(same128, a_full - a_bd, 0.0)
            l2 = jnp.where(same128, 0.0, a_full)

            y = -a_bd
            n_inv = y
            y = _bdot(y, y)
            for _ in range(4):
                st = _bdot(jnp.concatenate([n_inv, y], axis=0), y)
                n_inv = n_inv + y + st[:r]
                y = st[r:]
            n_inv = n_inv + y + _bdot(n_inv, y)
            for l_off in (l1, l2):
                m1 = l_off + _bdot(l_off, n_inv)
                n_inv = n_inv - (m1 + _bdot(n_inv, m1))

            e_gc = jnp.exp(gc_col)
            rhs = jnp.concatenate([v * beta, k * (beta * e_gc)], axis=1)
            uw = rhs + _bdot(n_inv, rhs)
            attn = jnp.where(earlier | (col_j == row_i), qk * dec, 0.0)
            qd = q * e_gc
            kd = k * jnp.exp(g_last - gc_col)

            u_sc[d, m] = uw[:, :HEAD_DIM]
            wq_sc[d, m] = jnp.concatenate([uw[:, HEAD_DIM:], qd], axis=0).astype(BF16)
            ak_sc[d, m] = jnp.concatenate([attn, kd.T], axis=0).astype(BF16)
            cd_sc[d, m] = jnp.broadcast_to(jnp.exp(g_last), (8, HEAD_DIM))
        return carry

    lax.fori_loop(0, n_blocks, prep, 0)

    oacc_sc[...] = jnp.zeros_like(oacc_sc)

    def scan(n, states):
        new_states = []
        for d, s in enumerate(states):
            m = n if d == 0 else n_blocks - 1 - n
            r1 = jnp.dot(wq_sc[d, m], s.astype(BF16), preferred_element_type=F32)
            v_new = u_sc[d, m] - r1[:r]
            r2 = jnp.dot(ak_sc[d, m], v_new.astype(BF16), preferred_element_type=F32)
            rows = pl.ds(pl.multiple_of(m * r, r), r)
            oacc_sc[rows, :] += r1[r:] + r2[:r]
            new_states.append(s * cd_sc[d, m][0:1, :] + r2[r:])
        return tuple(new_states)

    zero_state = jnp.zeros((HEAD_DIM, HEAD_DIM), F32)
    lax.fori_loop(0, n_blocks, scan, (zero_state, zero_state))

    def finish(m, carry):
        rows = pl.ds(pl.multiple_of(m * r, r), r)
        o = oacc_sc[rows, :]
        ms = jnp.mean(o * o, axis=-1, keepdims=True)
        o = o * lax.rsqrt(ms + RMS_EPS) * ong_ref[...]
        z = z_ref[rows, :].astype(F32)
        o_ref[rows, :] = (o * (z * jax.nn.sigmoid(z))).astype(o_ref.dtype)
        return carry

    lax.fori_loop(0, n_blocks, finish, 0)


def _dn_mixer(proj, gates, conv_w, a_log, dt_bias, out_g, batch, seq):
    r = DN_ROWS
    n_blocks = seq // r
    h = DN_HEADS
    proj3 = proj.reshape(batch, seq, proj.shape[1])
    gates3 = gates.reshape(batch, seq, HEAD_DIM)
    pad = HEAD_DIM - 2 * h
    alog = jnp.pad(a_log.reshape(1, 2 * h).astype(F32), ((0, 0), (0, pad)))
    dtb = jnp.pad(dt_bias.reshape(1, 2 * h).astype(F32), ((0, 0), (0, pad)))
    col = lambda off: pl.BlockSpec((None, seq, HEAD_DIM), lambda b, hd: (b, 0, off + hd))
    cw = lambda off: pl.BlockSpec((CONV_W, HEAD_DIM), lambda b, hd: (0, off + hd))
    row = pl.BlockSpec((1, HEAD_DIM), lambda b, hd: (0, 0))
    out = pl.pallas_call(
        functools.partial(_dn_kernel, n_blocks=n_blocks),
        grid=(batch, h),
        in_specs=[
            col(0), col(h), col(2 * h), col(3 * h),
            pl.BlockSpec((None, seq, HEAD_DIM), lambda b, hd: (b, 0, 0)),
            cw(0), cw(h), cw(2 * h),
            row, row, row,
        ],
        out_specs=pl.BlockSpec((None, seq, HEAD_DIM), lambda b, hd: (b, 0, hd)),
        out_shape=jax.ShapeDtypeStruct((batch, seq, MIX_W), BF16),
        scratch_shapes=[
            pltpu.VMEM((2, n_blocks, r, HEAD_DIM), F32),
            pltpu.VMEM((2, n_blocks, 2 * r, HEAD_DIM), BF16),
            pltpu.VMEM((2, n_blocks, r + HEAD_DIM, r), BF16),
            pltpu.VMEM((2, n_blocks, 8, HEAD_DIM), F32),
            pltpu.VMEM((seq, HEAD_DIM), F32),
        ],
        compiler_params=_cparams(("parallel", "arbitrary")),
        name="dn_mixer",
    )(proj3, proj3, proj3, proj3, gates3, conv_w, conv_w, conv_w, alog, dtb,
      out_g.reshape(1, HEAD_DIM).astype(F32))
    return out.reshape(batch * seq, MIX_W)


def kernel(x, mem, mem_norm_g, tok_norm_g, dn_w_in, dn_conv_w, dn_a_log, dn_dt_bias, dn_out_norm_g,
           gm_w_in, gm_v_norm_g, gm_w_s, gm_b_s, mem_w_kv, tok_w_out, ffn_norm_g, ffn_w1, ffn_w2,
           final_norm_g):
    batch, seq, d = x.shape
    t = batch * seq
    h = x.reshape(t, d)
    mem2 = mem.reshape(batch * N_MEM, d)
    n_gate = 4 * DN_HEADS
    for layer in range(DEPTH):
        j = layer // 2
        g_tok = tok_norm_g[layer]
        if layer % 2 == 0:
            w = dn_w_in[j]
            w_main = jnp.concatenate([w[:, :4 * MIX_W], w[:, 4 * MIX_W + n_gate:]], axis=1).astype(BF16)
            w_gate = jnp.pad(w[:, 4 * MIX_W:4 * MIX_W + n_gate],
                             ((0, 0), (0, HEAD_DIM - n_gate))).astype(BF16)
            proj = _norm_matmul(h, g_tok, w_main, BF16, 1024, 512)
            gates = _norm_matmul(h, g_tok, w_gate, F32, 1024, HEAD_DIM)
            mix = _dn_mixer(proj, gates, dn_conv_w[j], dn_a_log[j], dn_dt_bias[j], dn_out_norm_g[j],
                            batch, seq)
            q_block = 4 * MIX_W // MEM_W
        else:
            proj = _norm_matmul(h, g_tok, gm_w_in[j].astype(BF16), BF16, 1024, 512)
            mix = _gm_mixer(proj, gm_v_norm_g[j], gm_w_s[j].astype(BF16), gm_b_s[j].T.astype(F32), 512)
            q_block = 2 * MIX_W // MEM_W
        kv = _norm_matmul(mem2, mem_norm_g, mem_w_kv[layer].astype(BF16), BF16, batch * N_MEM, 512)
        mo = _mem_attn(proj, kv, batch, seq, q_block, 1024)
        w_out = tok_w_out[layer].astype(BF16)
        h = _out_proj(h, mix, mo, w_out[:MIX_W], w_out[MIX_W:], 1024, 512)
        h = _ffn(h, ffn_norm_g[layer], ffn_w1[layer].astype(BF16), ffn_w2[layer].astype(BF16),
                 final_norm_g, layer == DEPTH - 1, 512, 1024)
    return h.reshape(batch, seq, d)
```

```python
import functools

import jax
import jax.numpy as jnp
from jax import lax
from jax.experimental import pallas as pl
from jax.experimental.pallas import tpu as pltpu

D_MODEL = 2048
DEPTH = 4
N_MEM = 256
HEAD_DIM = 128
N_MEM_HEADS = 4
MEM_W = N_MEM_HEADS * HEAD_DIM
MIX_W = D_MODEL - MEM_W
DN_HEADS = MIX_W // HEAD_DIM
DN_CHUNK = 64
CONV_W = 5
GM_GROUPS = MIX_W // HEAD_DIM
GM_CHUNK = 128
FFN_HIDDEN = 4 * D_MODEL
RMS_EPS = 1e-6
L2_EPS = 1e-6

F32 = jnp.float32
BF16 = jnp.bfloat16

DN_ROWS = 256
DN_BASE = 8
DN_MERGE_SIZES = (8, 16, 32, 64)
DN_HEADS_PER_STEP = 2
CONV_HALO = 16
NORM_ROWS = 256
DN_GATES = 4 * DN_HEADS
DN_TAIL_W = MEM_W + HEAD_DIM
MASK_OFF = -1e30
VMEM_LIMIT = 56 * 1024 * 1024


def _cparams(semantics):
    return pltpu.CompilerParams(dimension_semantics=semantics, vmem_limit_bytes=VMEM_LIMIT)


def _rmsnorm_rows(x, g):
    ms = jnp.mean(x * x, axis=-1, keepdims=True)
    return x * lax.rsqrt(ms + RMS_EPS) * g


def _bdot(a, b):
    return jnp.dot(a, b, preferred_element_type=F32)


def _norm_matmul_kernel(x_ref, g_ref, w_ref, *rest, rows, n_main, has_tail):
    if has_tail:
        wt_ref, o_ref, ot_ref, xn_ref = rest
    else:
        o_ref, xn_ref = rest
    j = pl.program_id(1)

    @pl.when(j == 0)
    def _():
        def body(r, c):
            sl = pl.ds(pl.multiple_of(r * rows, rows), rows)
            xn_ref[sl, :] = _rmsnorm_rows(x_ref[sl, :], g_ref[...]).astype(xn_ref.dtype)
            return c

        lax.fori_loop(0, x_ref.shape[0] // rows, body, 0)

    @pl.when(j < n_main)
    def _():
        o_ref[...] = _bdot(xn_ref[...], w_ref[...].astype(BF16)).astype(o_ref.dtype)

    if has_tail:
        @pl.when(j == n_main)
        def _():
            ot_ref[...] = _bdot(xn_ref[...], wt_ref[...].astype(BF16))


def _norm_matmul(x, g, w_stack, layer, n_cols, tm, tn, w_tail=None):
    m, k = x.shape
    n_main = n_cols // tn
    has_tail = w_tail is not None
    last = n_main - 1
    in_specs = [
        pl.BlockSpec((tm, k), lambda i, j: (i, 0)),
        pl.BlockSpec((1, k), lambda i, j: (0, 0)),
        pl.BlockSpec((None, k, tn), lambda i, j: (layer, 0, jnp.minimum(j, last))),
    ]
    out_specs = [pl.BlockSpec((tm, tn), lambda i, j: (i, jnp.minimum(j, last)))]
    out_shape = [jax.ShapeDtypeStruct((m, n_cols), BF16)]
    args = [x, g.reshape(1, k), w_stack]
    if has_tail:
        nt = w_tail.shape[1]
        in_specs.append(pl.BlockSpec((k, nt), lambda i, j: (0, 0)))
        out_specs.append(pl.BlockSpec((tm, nt), lambda i, j: (i, 0)))
        out_shape.append(jax.ShapeDtypeStruct((m, nt), F32))
        args.append(w_tail)
    outs = pl.pallas_call(
        functools.partial(_norm_matmul_kernel, rows=min(NORM_ROWS, tm), n_main=n_main, has_tail=has_tail),
        grid=(m // tm, n_main + int(has_tail)),
        in_specs=in_specs,
        out_specs=out_specs,
        out_shape=out_shape,
        scratch_shapes=[pltpu.VMEM((tm, k), BF16)],
        compiler_params=_cparams(("parallel", "arbitrary")),
        name="norm_matmul",
    )(*args)
    return outs if has_tail else outs[0]


def _out_proj_kernel(h_ref, mix_ref, mo_ref, wa_ref, wb_ref, o_ref):
    acc = _bdot(mix_ref[...], wa_ref[...].astype(BF16))
    acc = acc + _bdot(mo_ref[...], wb_ref[...].astype(BF16))
    o_ref[...] = h_ref[...] + acc


def _out_proj(h, mix, mo, w_stack, layer, tm, tn):
    m, d = h.shape
    return pl.pallas_call(
        _out_proj_kernel,
        grid=(m // tm, d // tn),
        in_specs=[
            pl.BlockSpec((tm, tn), lambda i, j: (i, j)),
            pl.BlockSpec((tm, MIX_W), lambda i, j: (i, 0)),
            pl.BlockSpec((tm, MEM_W), lambda i, j: (i, 0)),
            pl.BlockSpec((None, MIX_W, tn), lambda i, j: (layer, 0, j)),
            pl.BlockSpec((None, MEM_W, tn), lambda i, j: (layer, MIX_W // MEM_W, j)),
        ],
        out_specs=pl.BlockSpec((tm, tn), lambda i, j: (i, j)),
        out_shape=jax.ShapeDtypeStruct((m, d), F32),
        compiler_params=_cparams(("parallel", "arbitrary")),
        name="out_proj",
    )(h, mix, mo, w_stack, w_stack)


def _ffn_kernel(h_ref, g_ref, w1_ref, w2_ref, fg_ref, o_ref, xn_ref, *, rows, final_norm):
    j = pl.program_id(1)
    n_rows = h_ref.shape[0] // rows

    @pl.when(j == 0)
    def _():
        def body(r, c):
            sl = pl.ds(pl.multiple_of(r * rows, rows), rows)
            x = h_ref[sl, :]
            xn_ref[sl, :] = _rmsnorm_rows(x, g_ref[...]).astype(xn_ref.dtype)
            o_ref[sl, :] = x
            return c

        lax.fori_loop(0, n_rows, body, 0)

    a = _bdot(xn_ref[...], w1_ref[...].astype(BF16))
    a = jnp.square(jnp.maximum(a, 0.0)).astype(BF16)
    o_ref[...] += _bdot(a, w2_ref[...].astype(BF16))

    if final_norm:
        @pl.when(j == pl.num_programs(1) - 1)
        def _():
            def body(r, c):
                sl = pl.ds(pl.multiple_of(r * rows, rows), rows)
                o_ref[sl, :] = _rmsnorm_rows(o_ref[sl, :], fg_ref[...])
                return c

            lax.fori_loop(0, n_rows, body, 0)


def _ffn(h, g, w1_stack, w2_stack, layer, fg, final_norm, tm, tf):
    m, d = h.shape
    f = w1_stack.shape[2]
    return pl.pallas_call(
        functools.partial(_ffn_kernel, rows=min(NORM_ROWS, tm), final_norm=final_norm),
        grid=(m // tm, f // tf),
        in_specs=[
            pl.BlockSpec((tm, d), lambda i, j: (i, 0), pipeline_mode=pl.Buffered(1)),
            pl.BlockSpec((1, d), lambda i, j: (0, 0)),
            pl.BlockSpec((None, d, tf), lambda i, j: (layer, 0, j)),
            pl.BlockSpec((None, tf, d), lambda i, j: (layer, j, 0)),
            pl.BlockSpec((1, d), lambda i, j: (0, 0)),
        ],
        out_specs=pl.BlockSpec((tm, d), lambda i, j: (i, 0)),
        out_shape=jax.ShapeDtypeStruct((m, d), F32),
        scratch_shapes=[pltpu.VMEM((tm, d), BF16)],
        compiler_params=_cparams(("parallel", "arbitrary")),
        name="ffn",
    )(h, g.reshape(1, d), w1_stack, w2_stack, fg.reshape(1, d))


def _mem_attn_kernel(q_ref, kv_ref, o_ref):
    scale = HEAD_DIM ** -0.5
    for hh in range(N_MEM_HEADS):
        lo, hi = hh * HEAD_DIM, (hh + 1) * HEAD_DIM
        q = q_ref[:, lo:hi].astype(BF16)
        k = kv_ref[:, lo:hi]
        v = kv_ref[:, MEM_W + lo:MEM_W + hi]
        s = lax.dot_general(q, k, (((1,), (1,)), ((), ())), preferred_element_type=F32) * scale
        s = s - jnp.max(s, axis=-1, keepdims=True)
        p = jnp.exp(s)
        l = jnp.sum(p, axis=-1, keepdims=True)
        o = _bdot(p.astype(BF16), v) / l
        o_ref[:, lo:hi] = o.astype(o_ref.dtype)


def _mem_attn(proj, kv, batch, seq, q_block, tq):
    n_q = seq // tq
    return pl.pallas_call(
        _mem_attn_kernel,
        grid=(batch, n_q),
        in_specs=[
            pl.BlockSpec((tq, MEM_W), lambda b, i: (b * n_q + i, q_block)),
            pl.BlockSpec((N_MEM, 2 * MEM_W), lambda b, i: (b, 0)),
        ],
        out_specs=pl.BlockSpec((tq, MEM_W), lambda b, i: (b * n_q + i, 0)),
        out_shape=jax.ShapeDtypeStruct((batch * seq, MEM_W), BF16),
        compiler_params=_cparams(("parallel", "arbitrary")),
        name="mem_attn",
    )(proj, kv)


def _gm_kernel(u_ref, v_ref, vg_ref, ws_ref, bs_ref, o_ref, *, n_chunks):
    v = jax.nn.gelu(v_ref[...].astype(F32))
    vn = _rmsnorm_rows(v, vg_ref[...]).astype(BF16)
    for g in range(GM_GROUPS):
        lo, hi = g * HEAD_DIM, (g + 1) * HEAD_DIM
        rhs = jnp.concatenate(
            [vn[c * GM_CHUNK:(c + 1) * GM_CHUNK, lo:hi] for c in range(n_chunks)], axis=1)
        sg = _bdot(ws_ref[g].astype(BF16), rhs) + bs_ref[:, g:g + 1]
        for c in range(n_chunks):
            rows = slice(c * GM_CHUNK, (c + 1) * GM_CHUNK)
            u = jax.nn.gelu(u_ref[rows, lo:hi].astype(F32))
            o_ref[rows, lo:hi] = (u * sg[:, c * HEAD_DIM:(c + 1) * HEAD_DIM]).astype(o_ref.dtype)


def _gm_mixer(proj, v_g, ws_stack, layer, b_s_t, ts):
    t = proj.shape[0]
    return pl.pallas_call(
        functools.partial(_gm_kernel, n_chunks=ts // GM_CHUNK),
        grid=(t // ts,),
        in_specs=[
            pl.BlockSpec((ts, MIX_W), lambda i: (i, 0)),
            pl.BlockSpec((ts, MIX_W), lambda i: (i, 1)),
            pl.BlockSpec((1, MIX_W), lambda i: (0, 0)),
            pl.BlockSpec((None, GM_GROUPS, GM_CHUNK, GM_CHUNK), lambda i: (layer, 0, 0, 0)),
            pl.BlockSpec((GM_CHUNK, GM_GROUPS), lambda i: (0, 0)),
        ],
        out_specs=pl.BlockSpec((ts, MIX_W), lambda i: (i, 0)),
        out_shape=jax.ShapeDtypeStruct((t, MIX_W), BF16),
        compiler_params=_cparams(("parallel",)),
        name="gm_mixer",
    )(proj, proj, v_g.reshape(1, MIX_W), ws_stack, b_s_t)


def _softplus(x):
    return jnp.maximum(x, 0.0) + jnp.log1p(jnp.exp(-jnp.abs(x)))


def _split3(x):
    hi = x.astype(BF16)
    r1 = x - hi.astype(F32)
    mid = r1.astype(BF16)
    lo = (r1 - mid.astype(F32)).astype(BF16)
    return hi, mid, lo


def _conv_silu(x_ref, cw_ref, m, n_blocks):
    r = DN_ROWS
    seq = x_ref.shape[0]
    row0 = pl.multiple_of(m * r, r)
    main = x_ref[pl.ds(row0, r), :].astype(F32)
    p0 = pl.multiple_of(jnp.maximum(row0 - CONV_HALO, 0), CONV_HALO)
    n0 = pl.multiple_of(jnp.minimum(row0 + r, seq - CONV_HALO), CONV_HALO)
    prev = jnp.where(m > 0, x_ref[pl.ds(p0, CONV_HALO), :].astype(F32), 0.0)
    nxt = jnp.where(m < n_blocks - 1, x_ref[pl.ds(n0, CONV_HALO), :].astype(F32), 0.0)
    ext = jnp.concatenate([prev, main, nxt], axis=0)
    n_ext = r + 2 * CONV_HALO
    acc = None
    for j in range(CONV_W):
        shift = (CONV_W // 2 - j) % n_ext
        shifted = ext if shift == 0 else pltpu.roll(ext, shift, 0)
        term = shifted[CONV_HALO:CONV_HALO + r, :] * cw_ref[j:j + 1, :]
        acc = term if acc is None else acc + term
    return acc * jax.nn.sigmoid(acc)


def _dn_kernel(q_ref, k_ref, v_ref, z_ref, gate_ref, cwq_ref, cwk_ref, cwv_ref, alog_ref, dtb_ref,
               ong_ref, o_ref, qs_sc, ks_sc, vs_sc, oacc_sc, nmask_sc, tri_sc, bmask_sc, *, n_blocks, hb):
    r = DN_ROWS
    hd = HEAD_DIM
    h2 = r // 2
    head0 = pl.program_id(1) * hb
    lanes = [slice(hh * hd, (hh + 1) * hd) for hh in range(hb)]
    halves = (slice(0, h2), slice(h2, r))

    row_i = lax.broadcasted_iota(jnp.int32, (r, r), 0)
    col_j = lax.broadcasted_iota(jnp.int32, (r, r), 1)
    nmask_sc[0] = jnp.where(col_j <= row_i, 0.0, MASK_OFF)
    nmask_sc[1] = jnp.where(col_j >= row_i, 0.0, MASK_OFF)
    tri_sc[...] = jnp.where(col_j <= row_i, 1.0, 0.0).astype(BF16)
    row_h = lax.broadcasted_iota(jnp.int32, (h2, h2), 0)
    col_h = lax.broadcasted_iota(jnp.int32, (h2, h2), 1)
    diag = row_h == col_h
    same = lambda s: (row_h // s) == (col_h // s)
    bmask_sc[0] = jnp.where(diag, 1.0, 0.0).astype(BF16)
    bmask_sc[1] = jnp.where(same(DN_BASE), jnp.where(diag, 0.0, -1.0), 0.0).astype(BF16)
    for lvl, s in enumerate(DN_MERGE_SIZES):
        bmask_sc[2 + lvl] = jnp.where(same(2 * s), jnp.where(same(s), 0.0, 1.0), 0.0).astype(BF16)

    lane = lax.broadcasted_iota(jnp.int32, (1, hd), 1)
    sub = lax.broadcasted_iota(jnp.int32, (hd, 1), 0)

    def pick_col(x, idx):
        return jnp.sum(jnp.where(lane == idx, x, 0.0), axis=1, keepdims=True)

    def pick_row(xt, idx):
        return jnp.sum(jnp.where(sub == idx, xt, 0.0), axis=0, keepdims=True)

    def prep(m, carry):
        rows = pl.ds(pl.multiple_of(m * r, r), r)
        q = _conv_silu(q_ref, cwq_ref, m, n_blocks)
        k = _conv_silu(k_ref, cwk_ref, m, n_blocks)
        v = _conv_silu(v_ref, cwv_ref, m, n_blocks)
        for ln in lanes:
            qh, kh = q[:, ln], k[:, ln]
            qh = qh * lax.rsqrt(jnp.sum(qh * qh, axis=-1, keepdims=True) + L2_EPS) * (hd ** -0.5)
            kh = kh * lax.rsqrt(jnp.sum(kh * kh, axis=-1, keepdims=True) + L2_EPS)
            qs_sc[rows, ln] = qh.astype(BF16)
            ks_sc[rows, ln] = kh.astype(BF16)
        vs_sc[rows, :] = v.astype(BF16)
        return carry

    lax.fori_loop(0, n_blocks, prep, 0)
    oacc_sc[...] = jnp.zeros_like(oacc_sc)

    def gate_block(m):
        rows = pl.ds(pl.multiple_of(m * r, r), r)
        graw = gate_ref[rows, :]
        gval = -jnp.exp(alog_ref[...]) * _softplus(graw + dtb_ref[...])
        bval = jax.nn.sigmoid(graw)
        hi, mid, lo = _split3(gval)
        cs = _bdot(tri_sc[...], jnp.concatenate([hi, mid, lo], axis=1))
        prefix = cs[:, :hd] + cs[:, hd:2 * hd] + cs[:, 2 * hd:]
        return gval, bval, prefix

    def scan(n, states):
        blocks = (n, n_blocks - 1 - n)
        gval_f, bval_f, gc_f = gate_block(blocks[0])
        gval_b, bval_b, pre_b = gate_block(blocks[1])
        gc_b = pre_b[r - 1:r, :] - pre_b + gval_b
        gcs = (gc_f, gc_b)
        gcts = (gc_f.T, gc_b.T)
        bvals = (bval_f, bval_b)
        units = [(hh, d) for hh in range(hb) for d in range(2)]
        nu = len(units)
        rows = [pl.ds(pl.multiple_of(blocks[d] * r, r), r) for _, d in units]

        gc_col, gc_row, beta, g_last, q_b, k_b, q, k, v, kbeta = ([] for _ in range(10))
        for (hh, d), rw in zip(units, rows):
            lg = d * DN_HEADS + head0 + hh
            gc_col.append(pick_col(gcs[d], lg))
            gc_row.append(pick_row(gcts[d], lg))
            beta.append(pick_col(bvals[d], 2 * DN_HEADS + lg))
            g_last.append(gc_col[-1][r - 1:r, :] if d == 0 else gc_col[-1][0:1, :])
            q_b.append(qs_sc[rw, lanes[hh]])
            k_b.append(ks_sc[rw, lanes[hh]])
            q.append(q_b[-1].astype(F32))
            k.append(k_b[-1].astype(F32))
            v.append(vs_sc[rw, lanes[hh]].astype(F32))
            kbeta.append(k[-1] * beta[-1])

        qkk = [lax.dot_general(jnp.concatenate([q_b[u], kbeta[u].astype(BF16)], axis=0), k_b[u],
                               (((1,), (1,)), ((), ())), preferred_element_type=F32) for u in range(nu)]
        dec = [jnp.exp(gc_col[u] - gc_row[u] + nmask_sc[units[u][1]]) for u in range(nu)]
        attn = [(qkk[u][:r] * dec[u]).astype(BF16) for u in range(nu)]
        a_b = [(qkk[u][r:] * dec[u]).astype(BF16) for u in range(nu)]

        hidx = [(u, hf) for u in range(nu) for hf in range(2)]
        a_h = [a_b[u][halves[hf], halves[hf]] for u, hf in hidx]
        y = [a * bmask_sc[1] for a in a_h]
        t_b = [yy + bmask_sc[0] for yy in y]
        t = [tb.astype(F32) for tb in t_b]
        y = [_bdot(yy, yy).astype(BF16) for yy in y]
        st = [_bdot(jnp.concatenate([tb, yy], axis=0), yy) for tb, yy in zip(t_b, y)]
        t = [tt + s[:h2] for tt, s in zip(t, st)]
        t_b = [tt.astype(BF16) for tt in t]
        y = [s[h2:].astype(BF16) for s in st]
        t = [tt + _bdot(tb, yy) for tt, tb, yy in zip(t, t_b, y)]
        t_b = [tt.astype(BF16) for tt in t]
        for lvl in range(len(DN_MERGE_SIZES)):
            m1 = [_bdot(a * bmask_sc[2 + lvl], tb).astype(BF16) for a, tb in zip(a_h, t_b)]
            t = [tt - _bdot(tb, mm) for tt, tb, mm in zip(t, t_b, m1)]
            t_b = [tt.astype(BF16) for tt in t]
        order = [(0, 1) if d == 0 else (1, 0) for _, d in units]
        p = [_bdot(a_b[u][halves[se], halves[fi]], t_b[2 * u + fi]).astype(BF16)
             for u, (fi, se) in enumerate(order)]
        x_b = [(-_bdot(t_b[2 * u + se], p[u])).astype(BF16) for u, (fi, se) in enumerate(order)]
        zero = jnp.zeros((h2, h2), BF16)
        t_full = []
        for u, (_, d) in enumerate(units):
            if d == 0:
                top = jnp.concatenate([t_b[2 * u], zero], axis=1)
                bot = jnp.concatenate([x_b[u], t_b[2 * u + 1]], axis=1)
            else:
                top = jnp.concatenate([t_b[2 * u], x_b[u]], axis=1)
                bot = jnp.concatenate([zero, t_b[2 * u + 1]], axis=1)
            t_full.append(jnp.concatenate([top, bot], axis=0))

        e_gc = [jnp.exp(gc_col[u]) for u in range(nu)]
        rhs = [jnp.concatenate([v[u] * beta[u], kbeta[u] * e_gc[u]], axis=1).astype(BF16) for u in range(nu)]
        uw = [_bdot(t_full[u], rhs[u]) for u in range(nu)]
        qd = [(q[u] * e_gc[u]).astype(BF16) for u in range(nu)]
        kdt = [(k[u] * jnp.exp(g_last[u] - gc_col[u])).T.astype(BF16) for u in range(nu)]

        r1 = [_bdot(jnp.concatenate([uw[u][:, hd:].astype(BF16), qd[u]], axis=0), states[u].astype(BF16))
              for u in range(nu)]
        v_new = [(uw[u][:, :hd] - r1[u][:r]).astype(BF16) for u in range(nu)]
        r2 = [_bdot(jnp.concatenate([attn[u], kdt[u]], axis=0), v_new[u]) for u in range(nu)]
        for u, (hh, _) in enumerate(units):
            oacc_sc[rows[u], lanes[hh]] += r1[u][r:] + r2[u][:r]
        return tuple(states[u] * jnp.exp(g_last[u]) + r2[u][r:] for u in range(nu))

    zero_state = jnp.zeros((hd, hd), F32)
    lax.fori_loop(0, n_blocks, scan, (zero_state,) * (2 * hb))

    def finish(m, carry):
        rows = pl.ds(pl.multiple_of(m * r, r), r)
        for ln in lanes:
            o = oacc_sc[rows, ln]
            ms = jnp.mean(o * o, axis=-1, keepdims=True)
            o = o * lax.rsqrt(ms + RMS_EPS) * ong_ref[...]
            z = z_ref[rows, ln].astype(F32)
            o_ref[rows, ln] = (o * (z * jax.nn.sigmoid(z))).astype(o_ref.dtype)
        return carry

    lax.fori_loop(0, n_blocks, finish, 0)


def _dn_mixer(proj, tail, conv_w_stack, layer, a_log, dt_bias, out_g, batch, seq):
    r = DN_ROWS
    hb = DN_HEADS_PER_STEP
    n_blocks = seq // r
    n_groups = DN_HEADS // hb
    width = hb * HEAD_DIM
    proj3 = proj.reshape(batch, seq, proj.shape[1])
    tail3 = tail.reshape(batch, seq, DN_TAIL_W)
    pad = HEAD_DIM - 2 * DN_HEADS
    alog = jnp.pad(a_log.reshape(1, 2 * DN_HEADS).astype(F32), ((0, 0), (0, pad)))
    dtb = jnp.pad(dt_bias.reshape(1, 2 * DN_HEADS).astype(F32), ((0, 0), (0, pad)))
    col = lambda off: pl.BlockSpec((None, seq, width), lambda b, g: (b, 0, off * n_groups + g))
    cw = lambda off: pl.BlockSpec((None, CONV_W, width), lambda b, g: (layer, 0, off * n_groups + g))
    row = pl.BlockSpec((1, HEAD_DIM), lambda b, g: (0, 0))
    out = pl.pallas_call(
        functools.partial(_dn_kernel, n_blocks=n_blocks, hb=hb),
        grid=(batch, n_groups),
        in_specs=[
            col(0), col(1), col(2), col(3),
            pl.BlockSpec((None, seq, HEAD_DIM), lambda b, g: (b, 0, MEM_W // HEAD_DIM)),
            cw(0), cw(1), cw(2),
            row, row, row,
        ],
        out_specs=pl.BlockSpec((None, seq, width), lambda b, g: (b, 0, g)),
        out_shape=jax.ShapeDtypeStruct((batch, seq, MIX_W), BF16),
        scratch_shapes=[
            pltpu.VMEM((seq, width), BF16),
            pltpu.VMEM((seq, width), BF16),
            pltpu.VMEM((seq, width), BF16),
            pltpu.VMEM((seq, width), F32),
            pltpu.VMEM((2, r, r), F32),
            pltpu.VMEM((r, r), BF16),
            pltpu.VMEM((2 + len(DN_MERGE_SIZES), r // 2, r // 2), BF16),
        ],
        compiler_params=_cparams(("parallel", "arbitrary")),
        name="dn_mixer",
    )(proj3, proj3, proj3, proj3, tail3, conv_w_stack, conv_w_stack, conv_w_stack, alog, dtb,
      out_g.reshape(1, HEAD_DIM).astype(F32))
    return out.reshape(batch * seq, MIX_W)


def kernel(x, mem, mem_norm_g, tok_norm_g, dn_w_in, dn_conv_w, dn_a_log, dn_dt_bias, dn_out_norm_g,
           gm_w_in, gm_v_norm_g, gm_w_s, gm_b_s, mem_w_kv, tok_w_out, ffn_norm_g, ffn_w1, ffn_w2,
           final_norm_g):
    batch, seq, d = x.shape
    t = batch * seq
    h = x.reshape(t, d)
    mem2 = mem.reshape(batch * N_MEM, d)
    for layer in range(DEPTH):
        j = layer // 2
        g_tok = tok_norm_g[layer]
        if layer % 2 == 0:
            w = dn_w_in[j]
            w_tail = jnp.concatenate([w[:, 4 * MIX_W + DN_GATES:], w[:, 4 * MIX_W:4 * MIX_W + DN_GATES],
                                      jnp.zeros((d, HEAD_DIM - DN_GATES), F32)], axis=1)
            proj, tail = _norm_matmul(h, g_tok, dn_w_in, j, 4 * MIX_W, 1024, 512, w_tail)
            mix = _dn_mixer(proj, tail, dn_conv_w, j, dn_a_log[j], dn_dt_bias[j], dn_out_norm_g[j],
                            batch, seq)
            q_src, q_block = tail, 0
        else:
            proj = _norm_matmul(h, g_tok, gm_w_in, j, 2 * MIX_W + MEM_W, 1024, 512)
            mix = _gm_mixer(proj, gm_v_norm_g[j], gm_w_s, j, gm_b_s[j].T.astype(F32), 512)
            q_src, q_block = proj, 2 * MIX_W // MEM_W
        kv = _norm_matmul(mem2, mem_norm_g, mem_w_kv, layer, 2 * MEM_W, batch * N_MEM, 512)
        mo = _mem_attn(q_src, kv, batch, seq, q_block, min(1024, seq))
        h = _out_proj(h, mix, mo, tok_w_out, layer, 1024, 512)
        h = _ffn(h, ffn_norm_g[layer], ffn_w1, ffn_w2, layer, final_norm_g, layer == DEPTH - 1, 1024, 512)
    return h.reshape(batch, seq, d)
```

```python
import functools

import jax
import jax.numpy as jnp
from jax import lax
from jax.experimental import pallas as pl
from jax.experimental.pallas import tpu as pltpu

D_MODEL = 2048
DEPTH = 4
N_MEM = 256
HEAD_DIM = 128
N_MEM_HEADS = 4
MEM_W = N_MEM_HEADS * HEAD_DIM
MIX_W = D_MODEL - MEM_W
DN_HEADS = MIX_W // HEAD_DIM
DN_CHUNK = 64
CONV_W = 5
GM_GROUPS = MIX_W // HEAD_DIM
GM_CHUNK = 128
FFN_HIDDEN = 4 * D_MODEL
RMS_EPS = 1e-6
L2_EPS = 1e-6

F32 = jnp.float32
BF16 = jnp.bfloat16

DN_ROWS = 256
DN_BASE = 8
DN_MERGE_SIZES = (8, 16, 32, 64)
DN_HEADS_PER_STEP = 3
CONV_HALO = 16
NORM_ROWS = 256
DN_GATES = 4 * DN_HEADS
DN_TAIL_W = MEM_W + HEAD_DIM
MASK_OFF = -1e30
VMEM_LIMIT = 56 * 1024 * 1024


def _cparams(semantics):
    return pltpu.CompilerParams(dimension_semantics=semantics, vmem_limit_bytes=VMEM_LIMIT)


def _rmsnorm_rows(x, g):
    ms = jnp.mean(x * x, axis=-1, keepdims=True)
    return x * lax.rsqrt(ms + RMS_EPS) * g


def _bdot(a, b):
    return jnp.dot(a, b, preferred_element_type=F32)


def _norm_matmul_kernel(x_ref, g_ref, w_ref, *rest, rows, n_main, has_tail):
    if has_tail:
        wt_ref, o_ref, ot_ref, xn_ref = rest
    else:
        o_ref, xn_ref = rest
    j = pl.program_id(1)

    @pl.when(j == 0)
    def _():
        def body(r, c):
            sl = pl.ds(pl.multiple_of(r * rows, rows), rows)
            xn_ref[sl, :] = _rmsnorm_rows(x_ref[sl, :], g_ref[...]).astype(xn_ref.dtype)
            return c

        lax.fori_loop(0, x_ref.shape[0] // rows, body, 0)

    @pl.when(j < n_main)
    def _():
        o_ref[...] = _bdot(xn_ref[...], w_ref[...].astype(BF16)).astype(o_ref.dtype)

    if has_tail:
        @pl.when(j == n_main)
        def _():
            ot_ref[...] = _bdot(xn_ref[...], wt_ref[...].astype(BF16))


def _norm_matmul(x, g, w_stack, layer, n_cols, tm, tn, w_tail=None):
    m, k = x.shape
    n_main = n_cols // tn
    has_tail = w_tail is not None
    last = n_main - 1
    in_specs = [
        pl.BlockSpec((tm, k), lambda i, j: (i, 0)),
        pl.BlockSpec((1, k), lambda i, j: (0, 0)),
        pl.BlockSpec((None, k, tn), lambda i, j: (layer, 0, jnp.minimum(j, last))),
    ]
    out_specs = [pl.BlockSpec((tm, tn), lambda i, j: (i, jnp.minimum(j, last)))]
    out_shape = [jax.ShapeDtypeStruct((m, n_cols), BF16)]
    args = [x, g.reshape(1, k), w_stack]
    if has_tail:
        nt = w_tail.shape[2]
        in_specs.append(pl.BlockSpec((None, k, nt), lambda i, j: (layer, 0, 0)))
        out_specs.append(pl.BlockSpec((tm, nt), lambda i, j: (i, 0)))
        out_shape.append(jax.ShapeDtypeStruct((m, nt), F32))
        args.append(w_tail)
    outs = pl.pallas_call(
        functools.partial(_norm_matmul_kernel, rows=min(NORM_ROWS, tm), n_main=n_main, has_tail=has_tail),
        grid=(m // tm, n_main + int(has_tail)),
        in_specs=in_specs,
        out_specs=out_specs,
        out_shape=out_shape,
        scratch_shapes=[pltpu.VMEM((tm, k), BF16)],
        compiler_params=_cparams(("parallel", "arbitrary")),
        name="norm_matmul",
    )(*args)
    return outs if has_tail else outs[0]


def _out_proj_kernel(h_ref, mix_ref, mo_ref, wa_ref, wb_ref, o_ref, wa_sc, wb_sc):
    @pl.when(pl.program_id(0) == 0)
    def _():
        wa_sc[...] = wa_ref[...].astype(BF16)
        wb_sc[...] = wb_ref[...].astype(BF16)

    acc = _bdot(mix_ref[...], wa_sc[...])
    acc = acc + _bdot(mo_ref[...], wb_sc[...])
    o_ref[...] = h_ref[...] + acc


def _out_proj(h, mix, mo, w_stack, layer, tm):
    m, d = h.shape
    return pl.pallas_call(
        _out_proj_kernel,
        grid=(m // tm,),
        in_specs=[
            pl.BlockSpec((tm, d), lambda i: (i, 0)),
            pl.BlockSpec((tm, MIX_W), lambda i: (i, 0)),
            pl.BlockSpec((tm, MEM_W), lambda i: (i, 0)),
            pl.BlockSpec((None, MIX_W, d), lambda i: (layer, 0, 0), pipeline_mode=pl.Buffered(1)),
            pl.BlockSpec((None, MEM_W, d), lambda i: (layer, MIX_W // MEM_W, 0), pipeline_mode=pl.Buffered(1)),
        ],
        out_specs=pl.BlockSpec((tm, d), lambda i: (i, 0)),
        out_shape=jax.ShapeDtypeStruct((m, d), F32),
        scratch_shapes=[pltpu.VMEM((MIX_W, d), BF16), pltpu.VMEM((MEM_W, d), BF16)],
        compiler_params=_cparams(("arbitrary",)),
        name="out_proj",
    )(h, mix, mo, w_stack, w_stack)


def _ffn_kernel(h_ref, g_ref, w1_ref, w2_ref, fg_ref, o_ref, xn_ref, *, rows, final_norm):
    j = pl.program_id(1)
    n_rows = h_ref.shape[0] // rows

    @pl.when(j == 0)
    def _():
        def body(r, c):
            sl = pl.ds(pl.multiple_of(r * rows, rows), rows)
            x = h_ref[sl, :]
            xn_ref[sl, :] = _rmsnorm_rows(x, g_ref[...]).astype(xn_ref.dtype)
            o_ref[sl, :] = x
            return c

        lax.fori_loop(0, n_rows, body, 0)

    a = _bdot(xn_ref[...], w1_ref[...].astype(BF16))
    a = jnp.square(jnp.maximum(a, 0.0)).astype(BF16)
    o_ref[...] += _bdot(a, w2_ref[...].astype(BF16))

    if final_norm:
        @pl.when(j == pl.num_programs(1) - 1)
        def _():
            def body(r, c):
                sl = pl.ds(pl.multiple_of(r * rows, rows), rows)
                o_ref[sl, :] = _rmsnorm_rows(o_ref[sl, :], fg_ref[...])
                return c

            lax.fori_loop(0, n_rows, body, 0)


def _ffn(h, g, w1_stack, w2_stack, layer, fg, final_norm, tm, tf):
    m, d = h.shape
    f = w1_stack.shape[2]
    return pl.pallas_call(
        functools.partial(_ffn_kernel, rows=min(NORM_ROWS, tm), final_norm=final_norm),
        grid=(m // tm, f // tf),
        in_specs=[
            pl.BlockSpec((tm, d), lambda i, j: (i, 0)),
            pl.BlockSpec((1, d), lambda i, j: (0, 0)),
            pl.BlockSpec((None, d, tf), lambda i, j: (layer, 0, j)),
            pl.BlockSpec((None, tf, d), lambda i, j: (layer, j, 0)),
            pl.BlockSpec((1, d), lambda i, j: (0, 0)),
        ],
        out_specs=pl.BlockSpec((tm, d), lambda i, j: (i, 0)),
        out_shape=jax.ShapeDtypeStruct((m, d), F32),
        scratch_shapes=[pltpu.VMEM((tm, d), BF16)],
        compiler_params=_cparams(("parallel", "arbitrary")),
        name="ffn",
    )(h, g.reshape(1, d), w1_stack, w2_stack, fg.reshape(1, d))


def _mem_attn_kernel(q_ref, kv_ref, o_ref):
    scale = HEAD_DIM ** -0.5
    for hh in range(N_MEM_HEADS):
        lo, hi = hh * HEAD_DIM, (hh + 1) * HEAD_DIM
        q = q_ref[:, lo:hi].astype(BF16)
        k = kv_ref[:, lo:hi]
        v = kv_ref[:, MEM_W + lo:MEM_W + hi]
        s = lax.dot_general(q, k, (((1,), (1,)), ((), ())), preferred_element_type=F32) * scale
        s = s - jnp.max(s, axis=-1, keepdims=True)
        p = jnp.exp(s)
        l = jnp.sum(p, axis=-1, keepdims=True)
        o = _bdot(p.astype(BF16), v) / l
        o_ref[:, lo:hi] = o.astype(o_ref.dtype)


def _mem_attn(proj, kv, batch, seq, q_block, tq):
    n_q = seq // tq
    return pl.pallas_call(
        _mem_attn_kernel,
        grid=(batch, n_q),
        in_specs=[
            pl.BlockSpec((tq, MEM_W), lambda b, i: (b * n_q + i, q_block)),
            pl.BlockSpec((N_MEM, 2 * MEM_W), lambda b, i: (b, 0)),
        ],
        out_specs=pl.BlockSpec((tq, MEM_W), lambda b, i: (b * n_q + i, 0)),
        out_shape=jax.ShapeDtypeStruct((batch * seq, MEM_W), BF16),
        compiler_params=_cparams(("parallel", "arbitrary")),
        name="mem_attn",
    )(proj, kv)


def _gm_kernel(u_ref, v_ref, vg_ref, ws_ref, bs_ref, o_ref, *, n_chunks):
    v = jax.nn.gelu(v_ref[...].astype(F32))
    vn = _rmsnorm_rows(v, vg_ref[...]).astype(BF16)
    for g in range(GM_GROUPS):
        lo, hi = g * HEAD_DIM, (g + 1) * HEAD_DIM
        rhs = jnp.concatenate(
            [vn[c * GM_CHUNK:(c + 1) * GM_CHUNK, lo:hi] for c in range(n_chunks)], axis=1)
        sg = _bdot(ws_ref[g].astype(BF16), rhs) + bs_ref[:, g:g + 1]
        for c in range(n_chunks):
            rows = slice(c * GM_CHUNK, (c + 1) * GM_CHUNK)
            u = jax.nn.gelu(u_ref[rows, lo:hi].astype(F32))
            o_ref[rows, lo:hi] = (u * sg[:, c * HEAD_DIM:(c + 1) * HEAD_DIM]).astype(o_ref.dtype)


def _gm_mixer(proj, v_g, ws_stack, layer, b_s_t, ts):
    t = proj.shape[0]
    return pl.pallas_call(
        functools.partial(_gm_kernel, n_chunks=ts // GM_CHUNK),
        grid=(t // ts,),
        in_specs=[
            pl.BlockSpec((ts, MIX_W), lambda i: (i, 0)),
            pl.BlockSpec((ts, MIX_W), lambda i: (i, 1)),
            pl.BlockSpec((1, MIX_W), lambda i: (0, 0)),
            pl.BlockSpec((None, GM_GROUPS, GM_CHUNK, GM_CHUNK), lambda i: (layer, 0, 0, 0)),
            pl.BlockSpec((GM_CHUNK, GM_GROUPS), lambda i: (0, 0)),
        ],
        out_specs=pl.BlockSpec((ts, MIX_W), lambda i: (i, 0)),
        out_shape=jax.ShapeDtypeStruct((t, MIX_W), BF16),
        compiler_params=_cparams(("parallel",)),
        name="gm_mixer",
    )(proj, proj, v_g.reshape(1, MIX_W), ws_stack, b_s_t)


def _softplus(x):
    return jnp.maximum(x, 0.0) + jnp.log1p(jnp.exp(-jnp.abs(x)))


def _split3(x):
    hi = x.astype(BF16)
    r1 = x - hi.astype(F32)
    mid = r1.astype(BF16)
    lo = (r1 - mid.astype(F32)).astype(BF16)
    return hi, mid, lo


def _conv_silu(x_ref, cw_ref, m, n_blocks):
    r = DN_ROWS
    seq = x_ref.shape[0]
    row0 = pl.multiple_of(m * r, r)
    main = x_ref[pl.ds(row0, r), :].astype(F32)
    p0 = pl.multiple_of(jnp.maximum(row0 - CONV_HALO, 0), CONV_HALO)
    n0 = pl.multiple_of(jnp.minimum(row0 + r, seq - CONV_HALO), CONV_HALO)
    prev = jnp.where(m > 0, x_ref[pl.ds(p0, CONV_HALO), :].astype(F32), 0.0)
    nxt = jnp.where(m < n_blocks - 1, x_ref[pl.ds(n0, CONV_HALO), :].astype(F32), 0.0)
    ext = jnp.concatenate([prev, main, nxt], axis=0)
    n_ext = r + 2 * CONV_HALO
    acc = None
    for j in range(CONV_W):
        shift = (CONV_W // 2 - j) % n_ext
        shifted = ext if shift == 0 else pltpu.roll(ext, shift, 0)
        term = shifted[CONV_HALO:CONV_HALO + r, :] * cw_ref[j:j + 1, :]
        acc = term if acc is None else acc + term
    return acc * jax.nn.sigmoid(acc)


def _dn_kernel(q_ref, k_ref, v_ref, z_ref, gate_ref, cwq_ref, cwk_ref, cwv_ref, alog_ref, dtb_ref,
               ong_ref, o_ref, qs_sc, ks_sc, vs_sc, oacc_sc, nmask_sc, tri_sc, bmask_sc, *, n_blocks, hb):
    r = DN_ROWS
    hd = HEAD_DIM
    h2 = r // 2
    head0 = pl.program_id(1) * hb
    lanes = [slice(hh * hd, (hh + 1) * hd) for hh in range(hb)]
    halves = (slice(0, h2), slice(h2, r))

    row_i = lax.broadcasted_iota(jnp.int32, (r, r), 0)
    col_j = lax.broadcasted_iota(jnp.int32, (r, r), 1)
    nmask_sc[0] = jnp.where(col_j <= row_i, 0.0, MASK_OFF)
    nmask_sc[1] = jnp.where(col_j >= row_i, 0.0, MASK_OFF)
    tri_sc[...] = jnp.where(col_j <= row_i, 1.0, 0.0).astype(BF16)
    row_h = lax.broadcasted_iota(jnp.int32, (h2, h2), 0)
    col_h = lax.broadcasted_iota(jnp.int32, (h2, h2), 1)
    diag = row_h == col_h
    same = lambda s: (row_h // s) == (col_h // s)
    bmask_sc[0] = jnp.where(diag, 1.0, 0.0).astype(BF16)
    bmask_sc[1] = jnp.where(same(DN_BASE), jnp.where(diag, 0.0, -1.0), 0.0).astype(BF16)
    for lvl, s in enumerate(DN_MERGE_SIZES):
        bmask_sc[2 + lvl] = jnp.where(same(2 * s), jnp.where(same(s), 0.0, 1.0), 0.0).astype(BF16)

    lane = lax.broadcasted_iota(jnp.int32, (1, hd), 1)
    sub = lax.broadcasted_iota(jnp.int32, (hd, 1), 0)

    def pick_col(x, idx):
        return jnp.sum(jnp.where(lane == idx, x, 0.0), axis=1, keepdims=True)

    def pick_row(xt, idx):
        return jnp.sum(jnp.where(sub == idx, xt, 0.0), axis=0, keepdims=True)

    def prep(m, carry):
        rows = pl.ds(pl.multiple_of(m * r, r), r)
        q = _conv_silu(q_ref, cwq_ref, m, n_blocks)
        k = _conv_silu(k_ref, cwk_ref, m, n_blocks)
        v = _conv_silu(v_ref, cwv_ref, m, n_blocks)
        for ln in lanes:
            qh, kh = q[:, ln], k[:, ln]
            qh = qh * lax.rsqrt(jnp.sum(qh * qh, axis=-1, keepdims=True) + L2_EPS) * (hd ** -0.5)
            kh = kh * lax.rsqrt(jnp.sum(kh * kh, axis=-1, keepdims=True) + L2_EPS)
            qs_sc[rows, ln] = qh.astype(BF16)
            ks_sc[rows, ln] = kh.astype(BF16)
        vs_sc[rows, :] = v.astype(BF16)
        return carry

    lax.fori_loop(0, n_blocks, prep, 0)
    oacc_sc[...] = jnp.zeros_like(oacc_sc)

    def gate_block(m):
        rows = pl.ds(pl.multiple_of(m * r, r), r)
        graw = gate_ref[rows, :]
        gval = -jnp.exp(alog_ref[...]) * _softplus(graw + dtb_ref[...])
        bval = jax.nn.sigmoid(graw)
        hi, mid, lo = _split3(gval)
        cs = _bdot(tri_sc[...], jnp.concatenate([hi, mid, lo], axis=1))
        prefix = cs[:, :hd] + cs[:, hd:2 * hd] + cs[:, 2 * hd:]
        return gval, bval, prefix

    def scan(n, states):
        blocks = (n, n_blocks - 1 - n)
        gval_f, bval_f, gc_f = gate_block(blocks[0])
        gval_b, bval_b, pre_b = gate_block(blocks[1])
        gc_b = pre_b[r - 1:r, :] - pre_b + gval_b
        gcs = (gc_f, gc_b)
        gcts = (gc_f.T, gc_b.T)
        bvals = (bval_f, bval_b)
        units = [(hh, d) for hh in range(hb) for d in range(2)]
        nu = len(units)
        rows = [pl.ds(pl.multiple_of(blocks[d] * r, r), r) for _, d in units]

        gc_col, gc_row, beta, g_last, q_b, k_b, q, k, v, kbeta = ([] for _ in range(10))
        for (hh, d), rw in zip(units, rows):
            lg = d * DN_HEADS + head0 + hh
            gc_col.append(pick_col(gcs[d], lg))
            gc_row.append(pick_row(gcts[d], lg))
            beta.append(pick_col(bvals[d], 2 * DN_HEADS + lg))
            g_last.append(gc_col[-1][r - 1:r, :] if d == 0 else gc_col[-1][0:1, :])
            q_b.append(qs_sc[rw, lanes[hh]])
            k_b.append(ks_sc[rw, lanes[hh]])
            q.append(q_b[-1].astype(F32))
            k.append(k_b[-1].astype(F32))
            v.append(vs_sc[rw, lanes[hh]].astype(F32))
            kbeta.append(k[-1] * beta[-1])

        qkk = [lax.dot_general(jnp.concatenate([q_b[u], kbeta[u].astype(BF16)], axis=0), k_b[u],
                               (((1,), (1,)), ((), ())), preferred_element_type=F32) for u in range(nu)]
        dec = [jnp.exp(gc_col[u] - gc_row[u] + nmask_sc[units[u][1]]) for u in range(nu)]
        attn = [(qkk[u][:r] * dec[u]).astype(BF16) for u in range(nu)]
        a_b = [(qkk[u][r:] * dec[u]).astype(BF16) for u in range(nu)]

        hidx = [(u, hf) for u in range(nu) for hf in range(2)]
        a_h = [a_b[u][halves[hf], halves[hf]] for u, hf in hidx]
        y = [a * bmask_sc[1] for a in a_h]
        t_b = [yy + bmask_sc[0] for yy in y]
        t = [tb.astype(F32) for tb in t_b]
        y = [_bdot(yy, yy).astype(BF16) for yy in y]
        st = [_bdot(jnp.concatenate([tb, yy], axis=0), yy) for tb, yy in zip(t_b, y)]
        t = [tt + s[:h2] for tt, s in zip(t, st)]
        t_b = [tt.astype(BF16) for tt in t]
        y = [s[h2:].astype(BF16) for s in st]
        t = [tt + _bdot(tb, yy) for tt, tb, yy in zip(t, t_b, y)]
        t_b = [tt.astype(BF16) for tt in t]
        for lvl in range(len(DN_MERGE_SIZES)):
            m1 = [_bdot(a * bmask_sc[2 + lvl], tb).astype(BF16) for a, tb in zip(a_h, t_b)]
            t = [tt - _bdot(tb, mm) for tt, tb, mm in zip(t, t_b, m1)]
            t_b = [tt.astype(BF16) for tt in t]
        order = [(0, 1) if d == 0 else (1, 0) for _, d in units]
        p = [_bdot(a_b[u][halves[se], halves[fi]], t_b[2 * u + fi]).astype(BF16)
             for u, (fi, se) in enumerate(order)]
        x_b = [(-_bdot(t_b[2 * u + se], p[u])).astype(BF16) for u, (fi, se) in enumerate(order)]
        zero = jnp.zeros((h2, h2), BF16)
        t_full = []
        for u, (_, d) in enumerate(units):
            if d == 0:
                top = jnp.concatenate([t_b[2 * u], zero], axis=1)
                bot = jnp.concatenate([x_b[u], t_b[2 * u + 1]], axis=1)
            else:
                top = jnp.concatenate([t_b[2 * u], x_b[u]], axis=1)
                bot = jnp.concatenate([zero, t_b[2 * u + 1]], axis=1)
            t_full.append(jnp.concatenate([top, bot], axis=0))

        e_gc = [jnp.exp(gc_col[u]) for u in range(nu)]
        rhs = [jnp.concatenate([v[u] * beta[u], kbeta[u] * e_gc[u]], axis=1).astype(BF16) for u in range(nu)]
        uw = [_bdot(t_full[u], rhs[u]) for u in range(nu)]
        qd = [(q[u] * e_gc[u]).astype(BF16) for u in range(nu)]
        kdt = [(k[u] * jnp.exp(g_last[u] - gc_col[u])).T.astype(BF16) for u in range(nu)]

        r1 = [_bdot(jnp.concatenate([uw[u][:, hd:].astype(BF16), qd[u]], axis=0), states[u].astype(BF16))
              for u in range(nu)]
        v_new = [(uw[u][:, :hd] - r1[u][:r]).astype(BF16) for u in range(nu)]
        r2 = [_bdot(jnp.concatenate([attn[u], kdt[u]], axis=0), v_new[u]) for u in range(nu)]
        for u, (hh, _) in enumerate(units):
            oacc_sc[rows[u], lanes[hh]] += r1[u][r:] + r2[u][:r]
        return tuple(states[u] * jnp.exp(g_last[u]) + r2[u][r:] for u in range(nu))

    zero_state = jnp.zeros((hd, hd), F32)
    lax.fori_loop(0, n_blocks, scan, (zero_state,) * (2 * hb))

    def finish(m, carry):
        rows = pl.ds(pl.multiple_of(m * r, r), r)
        for ln in lanes:
            o = oacc_sc[rows, ln]
            ms = jnp.mean(o * o, axis=-1, keepdims=True)
            o = o * lax.rsqrt(ms + RMS_EPS) * ong_ref[...]
            z = z_ref[rows, ln].astype(F32)
            o_ref[rows, ln] = (o * (z * jax.nn.sigmoid(z))).astype(o_ref.dtype)
        return carry

    lax.fori_loop(0, n_blocks, finish, 0)


def _dn_mixer(proj, tail, conv_w_stack, layer, a_log, dt_bias, out_g, batch, seq):
    r = DN_ROWS
    hb = DN_HEADS_PER_STEP
    n_blocks = seq // r
    n_groups = DN_HEADS // hb
    width = hb * HEAD_DIM
    proj3 = proj.reshape(batch, seq, proj.shape[1])
    tail3 = tail.reshape(batch, seq, DN_TAIL_W)
    pad = HEAD_DIM - 2 * DN_HEADS
    alog = jnp.pad(a_log.reshape(1, 2 * DN_HEADS).astype(F32), ((0, 0), (0, pad)))
    dtb = jnp.pad(dt_bias.reshape(1, 2 * DN_HEADS).astype(F32), ((0, 0), (0, pad)))
    col = lambda off: pl.BlockSpec((None, seq, width), lambda b, g: (b, 0, off * n_groups + g),
                                   pipeline_mode=pl.Buffered(1))
    cw = lambda off: pl.BlockSpec((None, CONV_W, width), lambda b, g: (layer, 0, off * n_groups + g))
    row = pl.BlockSpec((1, HEAD_DIM), lambda b, g: (0, 0))
    out = pl.pallas_call(
        functools.partial(_dn_kernel, n_blocks=n_blocks, hb=hb),
        grid=(batch, n_groups),
        in_specs=[
            col(0), col(1), col(2), col(3),
            pl.BlockSpec((None, seq, HEAD_DIM), lambda b, g: (b, 0, MEM_W // HEAD_DIM)),
            cw(0), cw(1), cw(2),
            row, row, row,
        ],
        out_specs=pl.BlockSpec((None, seq, width), lambda b, g: (b, 0, g)),
        out_shape=jax.ShapeDtypeStruct((batch, seq, MIX_W), BF16),
        scratch_shapes=[
            pltpu.VMEM((seq, width), BF16),
            pltpu.VMEM((seq, width), BF16),
            pltpu.VMEM((seq, width), BF16),
            pltpu.VMEM((seq, width), F32),
            pltpu.VMEM((2, r, r), F32),
            pltpu.VMEM((r, r), BF16),
            pltpu.VMEM((2 + len(DN_MERGE_SIZES), r // 2, r // 2), BF16),
        ],
        compiler_params=_cparams(("parallel", "arbitrary")),
        name="dn_mixer",
    )(proj3, proj3, proj3, proj3, tail3, conv_w_stack, conv_w_stack, conv_w_stack, alog, dtb,
      out_g.reshape(1, HEAD_DIM).astype(F32))
    return out.reshape(batch * seq, MIX_W)


def kernel(x, mem, mem_norm_g, tok_norm_g, dn_w_in, dn_conv_w, dn_a_log, dn_dt_bias, dn_out_norm_g,
           gm_w_in, gm_v_norm_g, gm_w_s, gm_b_s, mem_w_kv, tok_w_out, ffn_norm_g, ffn_w1, ffn_w2,
           final_norm_g):
    batch, seq, d = x.shape
    t = batch * seq
    h = x.reshape(t, d)
    mem2 = mem.reshape(batch * N_MEM, d)
    dn_w_main = dn_w_in[:, :, :4 * MIX_W]
    dn_w_tail = jnp.concatenate(
        [dn_w_in[:, :, 4 * MIX_W + DN_GATES:], dn_w_in[:, :, 4 * MIX_W:4 * MIX_W + DN_GATES],
         jnp.zeros(dn_w_in.shape[:2] + (HEAD_DIM - DN_GATES,), F32)], axis=2)
    for layer in range(DEPTH):
        j = layer // 2
        g_tok = tok_norm_g[layer]
        if layer % 2 == 0:
            proj, tail = _norm_matmul(h, g_tok, dn_w_main, j, 4 * MIX_W, 1024, 512, dn_w_tail)
            mix = _dn_mixer(proj, tail, dn_conv_w, j, dn_a_log[j], dn_dt_bias[j], dn_out_norm_g[j],
                            batch, seq)
            q_src, q_block = tail, 0
        else:
            proj = _norm_matmul(h, g_tok, gm_w_in, j, 2 * MIX_W + MEM_W, 1024, 512)
            mix = _gm_mixer(proj, gm_v_norm_g[j], gm_w_s, j, gm_b_s[j].T.astype(F32), 512)
            q_src, q_block = proj, 2 * MIX_W // MEM_W
        kv = _norm_matmul(mem2, mem_norm_g, mem_w_kv, layer, 2 * MEM_W, batch * N_MEM, 512)
        mo = _mem_attn(q_src, kv, batch, seq, q_block, min(1024, seq))
        h = _out_proj(h, mix, mo, tok_w_out, layer, 512)
        h = _ffn(h, ffn_norm_g[layer], ffn_w1, ffn_w2, layer, final_norm_g, layer == DEPTH - 1, 1024, 512)
    return h.reshape(batch, seq, d)
```

```python
import functools

import jax
import jax.numpy as jnp
from jax import lax
from jax.experimental import pallas as pl
from jax.experimental.pallas import tpu as pltpu

D_MODEL = 2048
DEPTH = 4
N_MEM = 256
HEAD_DIM = 128
N_MEM_HEADS = 4
MEM_W = N_MEM_HEADS * HEAD_DIM
MIX_W = D_MODEL - MEM_W
DN_HEADS = MIX_W // HEAD_DIM
DN_CHUNK = 64
CONV_W = 5
GM_GROUPS = MIX_W // HEAD_DIM
GM_CHUNK = 128
FFN_HIDDEN = 4 * D_MODEL
RMS_EPS = 1e-6
L2_EPS = 1e-6

F32 = jnp.float32
BF16 = jnp.bfloat16

DN_ROWS = 256
DN_BASE = 8
DN_MERGE_SIZES = (8, 16, 32, 64)
DN_HEADS_PER_STEP = 3
CONV_HALO = 16
NORM_ROWS = 256
DN_GATES = 4 * DN_HEADS
DN_TAIL_W = MEM_W + HEAD_DIM
MASK_OFF = -1e30
VMEM_LIMIT = 56 * 1024 * 1024


def _cparams(semantics):
    return pltpu.CompilerParams(dimension_semantics=semantics, vmem_limit_bytes=VMEM_LIMIT)


def _rmsnorm_rows(x, g):
    ms = jnp.mean(x * x, axis=-1, keepdims=True)
    return x * lax.rsqrt(ms + RMS_EPS) * g


def _bdot(a, b):
    return jnp.dot(a, b, preferred_element_type=F32)


def _norm_matmul_kernel(x_ref, g_ref, w_ref, *rest, rows, n_main, has_tail, w_transposed):
    if has_tail:
        wt_ref, o_ref, ot_ref, xn_ref = rest
    else:
        o_ref, xn_ref = rest
    j = pl.program_id(1)

    @pl.when(j == 0)
    def _():
        def body(r, c):
            sl = pl.ds(pl.multiple_of(r * rows, rows), rows)
            xn_ref[sl, :] = _rmsnorm_rows(x_ref[sl, :], g_ref[...]).astype(xn_ref.dtype)
            return c

        lax.fori_loop(0, x_ref.shape[0] // rows, body, 0)

    def proj(w):
        w = w.astype(BF16)
        if w_transposed:
            return lax.dot_general(xn_ref[...], w, (((1,), (1,)), ((), ())), preferred_element_type=F32)
        return _bdot(xn_ref[...], w)

    @pl.when(j < n_main)
    def _():
        o_ref[...] = proj(w_ref[...]).astype(o_ref.dtype)

    if has_tail:
        @pl.when(j == n_main)
        def _():
            ot_ref[...] = proj(wt_ref[...])


def _norm_matmul(x, g, w_stack, layer, n_cols, tm, tn, w_tail=None, w_transposed=False):
    m, k = x.shape
    n_main = n_cols // tn
    has_tail = w_tail is not None
    last = n_main - 1
    if w_transposed:
        w_spec = pl.BlockSpec((None, tn, k), lambda i, j: (layer, jnp.minimum(j, last), 0))
    else:
        w_spec = pl.BlockSpec((None, k, tn), lambda i, j: (layer, 0, jnp.minimum(j, last)))
    in_specs = [
        pl.BlockSpec((tm, k), lambda i, j: (i, 0)),
        pl.BlockSpec((1, k), lambda i, j: (0, 0)),
        w_spec,
    ]
    out_specs = [pl.BlockSpec((tm, tn), lambda i, j: (i, jnp.minimum(j, last)))]
    out_shape = [jax.ShapeDtypeStruct((m, n_cols), BF16)]
    args = [x, g.reshape(1, k), w_stack]
    if has_tail:
        nt = w_tail.shape[1 if w_transposed else 2]
        tail_block = (None, nt, k) if w_transposed else (None, k, nt)
        in_specs.append(pl.BlockSpec(tail_block, lambda i, j: (layer, 0, 0)))
        out_specs.append(pl.BlockSpec((tm, nt), lambda i, j: (i, 0)))
        out_shape.append(jax.ShapeDtypeStruct((m, nt), F32))
        args.append(w_tail)
    outs = pl.pallas_call(
        functools.partial(_norm_matmul_kernel, rows=min(NORM_ROWS, tm), n_main=n_main, has_tail=has_tail,
                          w_transposed=w_transposed),
        grid=(m // tm, n_main + int(has_tail)),
        in_specs=in_specs,
        out_specs=out_specs,
        out_shape=out_shape,
        scratch_shapes=[pltpu.VMEM((tm, k), BF16)],
        compiler_params=_cparams(("parallel", "arbitrary")),
        name="norm_matmul",
    )(*args)
    return outs if has_tail else outs[0]


def _out_proj_kernel(h_ref, mix_ref, mo_ref, wa_ref, wb_ref, o_ref, wa_sc, wb_sc):
    @pl.when(pl.program_id(0) == 0)
    def _():
        wa_sc[...] = wa_ref[...].astype(BF16)
        wb_sc[...] = wb_ref[...].astype(BF16)

    acc = _bdot(mix_ref[...], wa_sc[...])
    acc = acc + _bdot(mo_ref[...], wb_sc[...])
    o_ref[...] = h_ref[...] + acc


def _out_proj(h, mix, mo, w_stack, layer, tm):
    m, d = h.shape
    return pl.pallas_call(
        _out_proj_kernel,
        grid=(m // tm,),
        in_specs=[
            pl.BlockSpec((tm, d), lambda i: (i, 0)),
            pl.BlockSpec((tm, MIX_W), lambda i: (i, 0)),
            pl.BlockSpec((tm, MEM_W), lambda i: (i, 0)),
            pl.BlockSpec((None, MIX_W, d), lambda i: (layer, 0, 0), pipeline_mode=pl.Buffered(1)),
            pl.BlockSpec((None, MEM_W, d), lambda i: (layer, MIX_W // MEM_W, 0), pipeline_mode=pl.Buffered(1)),
        ],
        out_specs=pl.BlockSpec((tm, d), lambda i: (i, 0)),
        out_shape=jax.ShapeDtypeStruct((m, d), F32),
        scratch_shapes=[pltpu.VMEM((MIX_W, d), BF16), pltpu.VMEM((MEM_W, d), BF16)],
        compiler_params=_cparams(("arbitrary",)),
        name="out_proj",
    )(h, mix, mo, w_stack, w_stack)


def _ffn_kernel(h_ref, g_ref, w1_ref, w2_ref, fg_ref, o_ref, xn_ref, *, rows, final_norm):
    j = pl.program_id(1)
    n_rows = h_ref.shape[0] // rows

    @pl.when(j == 0)
    def _():
        def body(r, c):
            sl = pl.ds(pl.multiple_of(r * rows, rows), rows)
            x = h_ref[sl, :]
            xn_ref[sl, :] = _rmsnorm_rows(x, g_ref[...]).astype(xn_ref.dtype)
            o_ref[sl, :] = x
            return c

        lax.fori_loop(0, n_rows, body, 0)

    a = _bdot(xn_ref[...], w1_ref[...].astype(BF16))
    a = jnp.square(jnp.maximum(a, 0.0)).astype(BF16)
    o_ref[...] += _bdot(a, w2_ref[...].astype(BF16))

    if final_norm:
        @pl.when(j == pl.num_programs(1) - 1)
        def _():
            def body(r, c):
                sl = pl.ds(pl.multiple_of(r * rows, rows), rows)
                o_ref[sl, :] = _rmsnorm_rows(o_ref[sl, :], fg_ref[...])
                return c

            lax.fori_loop(0, n_rows, body, 0)


def _ffn(h, g, w1_stack, w2_stack, layer, fg, final_norm, tm, tf):
    m, d = h.shape
    f = w1_stack.shape[2]
    return pl.pallas_call(
        functools.partial(_ffn_kernel, rows=min(NORM_ROWS, tm), final_norm=final_norm),
        grid=(m // tm, f // tf),
        in_specs=[
            pl.BlockSpec((tm, d), lambda i, j: (i, 0)),
            pl.BlockSpec((1, d), lambda i, j: (0, 0)),
            pl.BlockSpec((None, d, tf), lambda i, j: (layer, 0, j)),
            pl.BlockSpec((None, tf, d), lambda i, j: (layer, j, 0)),
            pl.BlockSpec((1, d), lambda i, j: (0, 0)),
        ],
        out_specs=pl.BlockSpec((tm, d), lambda i, j: (i, 0)),
        out_shape=jax.ShapeDtypeStruct((m, d), F32),
        scratch_shapes=[pltpu.VMEM((tm, d), BF16)],
        compiler_params=_cparams(("parallel", "arbitrary")),
        name="ffn",
    )(h, g.reshape(1, d), w1_stack, w2_stack, fg.reshape(1, d))


def _mem_attn_kernel(q_ref, kv_ref, o_ref):
    scale = HEAD_DIM ** -0.5
    for hh in range(N_MEM_HEADS):
        lo, hi = hh * HEAD_DIM, (hh + 1) * HEAD_DIM
        q = q_ref[:, lo:hi].astype(BF16)
        k = kv_ref[:, lo:hi]
        v = kv_ref[:, MEM_W + lo:MEM_W + hi]
        s = lax.dot_general(q, k, (((1,), (1,)), ((), ())), preferred_element_type=F32) * scale
        s = s - jnp.max(s, axis=-1, keepdims=True)
        p = jnp.exp(s)
        l = jnp.sum(p, axis=-1, keepdims=True)
        o = _bdot(p.astype(BF16), v) / l
        o_ref[:, lo:hi] = o.astype(o_ref.dtype)


def _mem_attn(proj, kv, batch, seq, q_block, tq):
    n_q = seq // tq
    return pl.pallas_call(
        _mem_attn_kernel,
        grid=(batch, n_q),
        in_specs=[
            pl.BlockSpec((tq, MEM_W), lambda b, i: (b * n_q + i, q_block)),
            pl.BlockSpec((N_MEM, 2 * MEM_W), lambda b, i: (b, 0)),
        ],
        out_specs=pl.BlockSpec((tq, MEM_W), lambda b, i: (b * n_q + i, 0)),
        out_shape=jax.ShapeDtypeStruct((batch * seq, MEM_W), BF16),
        compiler_params=_cparams(("parallel", "arbitrary")),
        name="mem_attn",
    )(proj, kv)


def _gm_kernel(u_ref, v_ref, vg_ref, ws_ref, bs_ref, o_ref, *, n_chunks):
    v = jax.nn.gelu(v_ref[...].astype(F32))
    vn = _rmsnorm_rows(v, vg_ref[...]).astype(BF16)
    for g in range(GM_GROUPS):
        lo, hi = g * HEAD_DIM, (g + 1) * HEAD_DIM
        rhs = jnp.concatenate(
            [vn[c * GM_CHUNK:(c + 1) * GM_CHUNK, lo:hi] for c in range(n_chunks)], axis=1)
        sg = _bdot(ws_ref[g].astype(BF16), rhs) + bs_ref[:, g:g + 1]
        for c in range(n_chunks):
            rows = slice(c * GM_CHUNK, (c + 1) * GM_CHUNK)
            u = jax.nn.gelu(u_ref[rows, lo:hi].astype(F32))
            o_ref[rows, lo:hi] = (u * sg[:, c * HEAD_DIM:(c + 1) * HEAD_DIM]).astype(o_ref.dtype)


def _gm_mixer(proj, v_g, ws_stack, layer, b_s_t, ts):
    t = proj.shape[0]
    return pl.pallas_call(
        functools.partial(_gm_kernel, n_chunks=ts // GM_CHUNK),
        grid=(t // ts,),
        in_specs=[
            pl.BlockSpec((ts, MIX_W), lambda i: (i, 0)),
            pl.BlockSpec((ts, MIX_W), lambda i: (i, 1)),
            pl.BlockSpec((1, MIX_W), lambda i: (0, 0)),
            pl.BlockSpec((None, GM_GROUPS, GM_CHUNK, GM_CHUNK), lambda i: (layer, 0, 0, 0)),
            pl.BlockSpec((GM_CHUNK, GM_GROUPS), lambda i: (0, 0)),
        ],
        out_specs=pl.BlockSpec((ts, MIX_W), lambda i: (i, 0)),
        out_shape=jax.ShapeDtypeStruct((t, MIX_W), BF16),
        compiler_params=_cparams(("parallel",)),
        name="gm_mixer",
    )(proj, proj, v_g.reshape(1, MIX_W), ws_stack, b_s_t)


def _softplus(x):
    return jnp.maximum(x, 0.0) + jnp.log1p(jnp.exp(-jnp.abs(x)))


def _split3(x):
    hi = x.astype(BF16)
    r1 = x - hi.astype(F32)
    mid = r1.astype(BF16)
    lo = (r1 - mid.astype(F32)).astype(BF16)
    return hi, mid, lo


def _conv_silu(x_ref, cw_ref, shift_ref, m, n_blocks):
    r = DN_ROWS
    seq = x_ref.shape[0]
    row0 = pl.multiple_of(m * r, r)
    main = x_ref[pl.ds(row0, r), :]
    p0 = pl.multiple_of(jnp.maximum(row0 - CONV_HALO, 0), CONV_HALO)
    n0 = pl.multiple_of(jnp.minimum(row0 + r, seq - CONV_HALO), CONV_HALO)
    zero = jnp.zeros((CONV_HALO, x_ref.shape[1]), x_ref.dtype)
    prev = jnp.where(m > 0, x_ref[pl.ds(p0, CONV_HALO), :], zero)
    nxt = jnp.where(m < n_blocks - 1, x_ref[pl.ds(n0, CONV_HALO), :], zero)
    ext = jnp.concatenate([prev, main, nxt], axis=0)
    centre = CONV_W // 2
    h2 = r // 2
    halves = []
    for row in (0, h2):
        shifted = _bdot(shift_ref[...], ext[row:row + h2 + 2 * CONV_HALO, :])
        acc = main[row:row + h2, :].astype(F32) * cw_ref[centre:centre + 1, :]
        for t, j in enumerate(j for j in range(CONV_W) if j != centre):
            acc = acc + shifted[t * h2:(t + 1) * h2, :] * cw_ref[j:j + 1, :]
        halves.append(acc)
    acc = jnp.concatenate(halves, axis=0)
    return acc * jax.nn.sigmoid(acc)


def _dn_kernel(q_ref, k_ref, v_ref, z_ref, gate_ref, cwq_ref, cwk_ref, cwv_ref, alog_ref, dtb_ref,
               ong_ref, o_ref, qs_sc, ks_sc, vs_sc, oacc_sc, nmask_sc, tri_sc, bmask_sc, shift_sc, *,
               n_blocks, hb):
    r = DN_ROWS
    hd = HEAD_DIM
    h2 = r // 2
    head0 = pl.program_id(1) * hb
    lanes = [slice(hh * hd, (hh + 1) * hd) for hh in range(hb)]
    halves = (slice(0, h2), slice(h2, r))

    row_i = lax.broadcasted_iota(jnp.int32, (r, r), 0)
    col_j = lax.broadcasted_iota(jnp.int32, (r, r), 1)
    nmask_sc[0] = jnp.where(col_j <= row_i, 0.0, MASK_OFF)
    nmask_sc[1] = jnp.where(col_j >= row_i, 0.0, MASK_OFF)
    tri_sc[...] = jnp.where(col_j <= row_i, 1.0, 0.0).astype(BF16)
    row_h = lax.broadcasted_iota(jnp.int32, (h2, h2), 0)
    col_h = lax.broadcasted_iota(jnp.int32, (h2, h2), 1)
    diag = row_h == col_h
    same = lambda s: (row_h // s) == (col_h // s)
    bmask_sc[0] = jnp.where(diag, 1.0, 0.0).astype(BF16)
    bmask_sc[1] = jnp.where(same(DN_BASE), jnp.where(diag, 0.0, -1.0), 0.0).astype(BF16)
    for lvl, s in enumerate(DN_MERGE_SIZES):
        bmask_sc[2 + lvl] = jnp.where(same(2 * s), jnp.where(same(s), 0.0, 1.0), 0.0).astype(BF16)

    n_taps = CONV_W - 1
    out_row = lax.broadcasted_iota(jnp.int32, (n_taps * h2, h2 + 2 * CONV_HALO), 0)
    src_row = lax.broadcasted_iota(jnp.int32, (n_taps * h2, h2 + 2 * CONV_HALO), 1)
    tap = out_row // h2
    offset = tap - CONV_W // 2 + jnp.where(tap >= CONV_W // 2, 1, 0)
    shift_sc[...] = jnp.where(src_row == out_row - tap * h2 + CONV_HALO + offset, 1.0, 0.0).astype(BF16)

    lane = lax.broadcasted_iota(jnp.int32, (1, hd), 1)
    sub = lax.broadcasted_iota(jnp.int32, (hd, 1), 0)

    def pick_col(x, idx):
        return jnp.sum(jnp.where(lane == idx, x, 0.0), axis=1, keepdims=True)

    def pick_row(xt, idx):
        return jnp.sum(jnp.where(sub == idx, xt, 0.0), axis=0, keepdims=True)

    def prep(m, carry):
        rows = pl.ds(pl.multiple_of(m * r, r), r)
        q = _conv_silu(q_ref, cwq_ref, shift_sc, m, n_blocks)
        k = _conv_silu(k_ref, cwk_ref, shift_sc, m, n_blocks)
        v = _conv_silu(v_ref, cwv_ref, shift_sc, m, n_blocks)
        for ln in lanes:
            qh, kh = q[:, ln], k[:, ln]
            qh = qh * lax.rsqrt(jnp.sum(qh * qh, axis=-1, keepdims=True) + L2_EPS) * (hd ** -0.5)
            kh = kh * lax.rsqrt(jnp.sum(kh * kh, axis=-1, keepdims=True) + L2_EPS)
            qs_sc[rows, ln] = qh.astype(BF16)
            ks_sc[rows, ln] = kh.astype(BF16)
        vs_sc[rows, :] = v.astype(BF16)
        return carry

    lax.fori_loop(0, n_blocks, prep, 0)
    oacc_sc[...] = jnp.zeros_like(oacc_sc)

    def gate_block(m):
        rows = pl.ds(pl.multiple_of(m * r, r), r)
        graw = gate_ref[rows, :]
        gval = -jnp.exp(alog_ref[...]) * _softplus(graw + dtb_ref[...])
        bval = jax.nn.sigmoid(graw)
        hi, mid, lo = _split3(gval)
        cs = _bdot(tri_sc[...], jnp.concatenate([hi, mid, lo], axis=1))
        prefix = cs[:, :hd] + cs[:, hd:2 * hd] + cs[:, 2 * hd:]
        return gval, bval, prefix

    def scan(n, states):
        blocks = (n, n_blocks - 1 - n)
        gval_f, bval_f, gc_f = gate_block(blocks[0])
        gval_b, bval_b, pre_b = gate_block(blocks[1])
        gc_b = pre_b[r - 1:r, :] - pre_b + gval_b
        gcs = (gc_f, gc_b)
        gcts = (gc_f.T, gc_b.T)
        bvals = (bval_f, bval_b)
        units = [(hh, d) for hh in range(hb) for d in range(2)]
        nu = len(units)
        rows = [pl.ds(pl.multiple_of(blocks[d] * r, r), r) for _, d in units]

        gc_col, gc_row, beta, g_last, q_b, k_b, q, k, v, kbeta = ([] for _ in range(10))
        for (hh, d), rw in zip(units, rows):
            lg = d * DN_HEADS + head0 + hh
            gc_col.append(pick_col(gcs[d], lg))
            gc_row.append(pick_row(gcts[d], lg))
            beta.append(pick_col(bvals[d], 2 * DN_HEADS + lg))
            g_last.append(gc_col[-1][r - 1:r, :] if d == 0 else gc_col[-1][0:1, :])
            q_b.append(qs_sc[rw, lanes[hh]])
            k_b.append(ks_sc[rw, lanes[hh]])
            q.append(q_b[-1].astype(F32))
            k.append(k_b[-1].astype(F32))
            v.append(vs_sc[rw, lanes[hh]].astype(F32))
            kbeta.append(k[-1] * beta[-1])

        qkk = [lax.dot_general(jnp.concatenate([q_b[u], kbeta[u].astype(BF16)], axis=0), k_b[u],
                               (((1,), (1,)), ((), ())), preferred_element_type=F32) for u in range(nu)]
        dec = [jnp.exp(gc_col[u] - gc_row[u] + nmask_sc[units[u][1]]) for u in range(nu)]
        attn = [(qkk[u][:r] * dec[u]).astype(BF16) for u in range(nu)]
        a_b = [(qkk[u][r:] * dec[u]).astype(BF16) for u in range(nu)]

        hidx = [(u, hf) for u in range(nu) for hf in range(2)]
        a_h = [a_b[u][halves[hf], halves[hf]] for u, hf in hidx]
        y = [a * bmask_sc[1] for a in a_h]
        t_b = [yy + bmask_sc[0] for yy in y]
        t = [tb.astype(F32) for tb in t_b]
        y = [_bdot(yy, yy).astype(BF16) for yy in y]
        st = [_bdot(jnp.concatenate([tb, yy], axis=0), yy) for tb, yy in zip(t_b, y)]
        t = [tt + s[:h2] for tt, s in zip(t, st)]
        t_b = [tt.astype(BF16) for tt in t]
        y = [s[h2:].astype(BF16) for s in st]
        t = [tt + _bdot(tb, yy) for tt, tb, yy in zip(t, t_b, y)]
        t_b = [tt.astype(BF16) for tt in t]
        for lvl in range(len(DN_MERGE_SIZES)):
            m1 = [_bdot(a * bmask_sc[2 + lvl], tb).astype(BF16) for a, tb in zip(a_h, t_b)]
            t = [tt - _bdot(tb, mm) for tt, tb, mm in zip(t, t_b, m1)]
            t_b = [tt.astype(BF16) for tt in t]
        order = [(0, 1) if d == 0 else (1, 0) for _, d in units]
        p = [_bdot(a_b[u][halves[se], halves[fi]], t_b[2 * u + fi]).astype(BF16)
             for u, (fi, se) in enumerate(order)]
        x_b = [(-_bdot(t_b[2 * u + se], p[u])).astype(BF16) for u, (fi, se) in enumerate(order)]
        zero = jnp.zeros((h2, h2), BF16)
        t_full = []
        for u, (_, d) in enumerate(units):
            if d == 0:
                top = jnp.concatenate([t_b[2 * u], zero], axis=1)
                bot = jnp.concatenate([x_b[u], t_b[2 * u + 1]], axis=1)
            else:
                top = jnp.concatenate([t_b[2 * u], x_b[u]], axis=1)
                bot = jnp.concatenate([zero, t_b[2 * u + 1]], axis=1)
            t_full.append(jnp.concatenate([top, bot], axis=0))

        e_gc = [jnp.exp(gc_col[u]) for u in range(nu)]
        rhs = [jnp.concatenate([v[u] * beta[u], kbeta[u] * e_gc[u]], axis=1).astype(BF16) for u in range(nu)]
        uw = [_bdot(t_full[u], rhs[u]) for u in range(nu)]
        qd = [(q[u] * e_gc[u]).astype(BF16) for u in range(nu)]
        kdt = [(k[u] * jnp.exp(g_last[u] - gc_col[u])).T.astype(BF16) for u in range(nu)]

        r1 = [_bdot(jnp.concatenate([uw[u][:, hd:].astype(BF16), qd[u]], axis=0), states[u].astype(BF16))
              for u in range(nu)]
        v_new = [(uw[u][:, :hd] - r1[u][:r]).astype(BF16) for u in range(nu)]
        r2 = [_bdot(jnp.concatenate([attn[u], kdt[u]], axis=0), v_new[u]) for u in range(nu)]
        for u, (hh, _) in enumerate(units):
            oacc_sc[rows[u], lanes[hh]] += r1[u][r:] + r2[u][:r]
        return tuple(states[u] * jnp.exp(g_last[u]) + r2[u][r:] for u in range(nu))

    zero_state = jnp.zeros((hd, hd), F32)
    lax.fori_loop(0, n_blocks, scan, (zero_state,) * (2 * hb))

    def finish(m, carry):
        rows = pl.ds(pl.multiple_of(m * r, r), r)
        for ln in lanes:
            o = oacc_sc[rows, ln]
            ms = jnp.mean(o * o, axis=-1, keepdims=True)
            o = o * lax.rsqrt(ms + RMS_EPS) * ong_ref[...]
            z = z_ref[rows, ln].astype(F32)
            o_ref[rows, ln] = (o * (z * jax.nn.sigmoid(z))).astype(o_ref.dtype)
        return carry

    lax.fori_loop(0, n_blocks, finish, 0)


def _dn_mixer(proj, tail, conv_w_stack, layer, a_log, dt_bias, out_g, batch, seq):
    r = DN_ROWS
    hb = DN_HEADS_PER_STEP
    n_blocks = seq // r
    n_groups = DN_HEADS // hb
    width = hb * HEAD_DIM
    proj3 = proj.reshape(batch, seq, proj.shape[1])
    tail3 = tail.reshape(batch, seq, DN_TAIL_W)
    pad = HEAD_DIM - 2 * DN_HEADS
    alog = jnp.pad(a_log.reshape(1, 2 * DN_HEADS).astype(F32), ((0, 0), (0, pad)))
    dtb = jnp.pad(dt_bias.reshape(1, 2 * DN_HEADS).astype(F32), ((0, 0), (0, pad)))
    col = lambda off: pl.BlockSpec((None, seq, width), lambda b, g: (b, 0, off * n_groups + g),
                                   pipeline_mode=pl.Buffered(1))
    cw = lambda off: pl.BlockSpec((None, CONV_W, width), lambda b, g: (layer, 0, off * n_groups + g))
    row = pl.BlockSpec((1, HEAD_DIM), lambda b, g: (0, 0))
    out = pl.pallas_call(
        functools.partial(_dn_kernel, n_blocks=n_blocks, hb=hb),
        grid=(batch, n_groups),
        in_specs=[
            col(0), col(1), col(2), col(3),
            pl.BlockSpec((None, seq, HEAD_DIM), lambda b, g: (b, 0, MEM_W // HEAD_DIM)),
            cw(0), cw(1), cw(2),
            row, row, row,
        ],
        out_specs=pl.BlockSpec((None, seq, width), lambda b, g: (b, 0, g)),
        out_shape=jax.ShapeDtypeStruct((batch, seq, MIX_W), BF16),
        scratch_shapes=[
            pltpu.VMEM((seq, width), BF16),
            pltpu.VMEM((seq, width), BF16),
            pltpu.VMEM((seq, width), BF16),
            pltpu.VMEM((seq, width), F32),
            pltpu.VMEM((2, r, r), F32),
            pltpu.VMEM((r, r), BF16),
            pltpu.VMEM((2 + len(DN_MERGE_SIZES), r // 2, r // 2), BF16),
            pltpu.VMEM(((CONV_W - 1) * (r // 2), r // 2 + 2 * CONV_HALO), BF16),
        ],
        compiler_params=_cparams(("parallel", "arbitrary")),
        name="dn_mixer",
    )(proj3, proj3, proj3, proj3, tail3, conv_w_stack, conv_w_stack, conv_w_stack, alog, dtb,
      out_g.reshape(1, HEAD_DIM).astype(F32))
    return out.reshape(batch * seq, MIX_W)


def kernel(x, mem, mem_norm_g, tok_norm_g, dn_w_in, dn_conv_w, dn_a_log, dn_dt_bias, dn_out_norm_g,
           gm_w_in, gm_v_norm_g, gm_w_s, gm_b_s, mem_w_kv, tok_w_out, ffn_norm_g, ffn_w1, ffn_w2,
           final_norm_g):
    batch, seq, d = x.shape
    t = batch * seq
    h = x.reshape(t, d)
    mem2 = mem.reshape(batch * N_MEM, d)
    dn_w_t = jnp.swapaxes(dn_w_in, 1, 2)
    dn_w_tail = jnp.concatenate(
        [dn_w_t[:, 4 * MIX_W + DN_GATES:], dn_w_t[:, 4 * MIX_W:4 * MIX_W + DN_GATES],
         jnp.zeros((dn_w_t.shape[0], HEAD_DIM - DN_GATES, d), F32)], axis=1)
    for layer in range(DEPTH):
        j = layer // 2
        g_tok = tok_norm_g[layer]
        if layer % 2 == 0:
            proj, tail = _norm_matmul(h, g_tok, dn_w_t, j, 4 * MIX_W, 1024, 512, dn_w_tail, w_transposed=True)
            mix = _dn_mixer(proj, tail, dn_conv_w, j, dn_a_log[j], dn_dt_bias[j], dn_out_norm_g[j],
                            batch, seq)
            q_src, q_block = tail, 0
        else:
            proj = _norm_matmul(h, g_tok, gm_w_in, j, 2 * MIX_W + MEM_W, 1024, 512)
            mix = _gm_mixer(proj, gm_v_norm_g[j], gm_w_s, j, gm_b_s[j].T.astype(F32), 512)
            q_src, q_block = proj, 2 * MIX_W // MEM_W
        kv = _norm_matmul(mem2, mem_norm_g, mem_w_kv, layer, 2 * MEM_W, batch * N_MEM, 512)
        mo = _mem_attn(q_src, kv, batch, seq, q_block, min(1024, seq))
        h = _out_proj(h, mix, mo, tok_w_out, layer, 512)
        h = _ffn(h, ffn_norm_g[layer], ffn_w1, ffn_w2, layer, final_norm_g, layer == DEPTH - 1, 1024, 512)
    return h.reshape(batch, seq, d)
```

```python
import functools

import jax
import jax.numpy as jnp
from jax import lax
from jax.experimental import pallas as pl
from jax.experimental.pallas import tpu as pltpu

D_MODEL = 2048
DEPTH = 4
N_MEM = 256
HEAD_DIM = 128
N_MEM_HEADS = 4
MEM_W = N_MEM_HEADS * HEAD_DIM
MIX_W = D_MODEL - MEM_W
DN_HEADS = MIX_W // HEAD_DIM
DN_CHUNK = 64
CONV_W = 5
GM_GROUPS = MIX_W // HEAD_DIM
GM_CHUNK = 128
FFN_HIDDEN = 4 * D_MODEL
RMS_EPS = 1e-6
L2_EPS = 1e-6

F32 = jnp.float32
BF16 = jnp.bfloat16

DN_ROWS = 256
DN_BASE = 16
DN_MERGE_SIZES = (16, 32, 64)
DN_BLOCKS_PER_ITER = 2
DN_HEADS_PER_STEP = 3
CONV_HALO = 16
NORM_ROWS = 256
DN_GATES = 4 * DN_HEADS
DN_TAIL_W = MEM_W + HEAD_DIM
MASK_OFF = -1e30
VMEM_LIMIT = 56 * 1024 * 1024


def _cparams(semantics):
    return pltpu.CompilerParams(dimension_semantics=semantics, vmem_limit_bytes=VMEM_LIMIT)


def _rmsnorm_rows(x, g):
    ms = jnp.mean(x * x, axis=-1, keepdims=True)
    return x * lax.rsqrt(ms + RMS_EPS) * g


def _bdot(a, b):
    return jnp.dot(a, b, preferred_element_type=F32)


def _norm_matmul_kernel(x_ref, g_ref, w_ref, *rest, rows, n_main, has_tail, w_transposed):
    if has_tail:
        wt_ref, o_ref, ot_ref, xn_ref = rest
    else:
        o_ref, xn_ref = rest
    j = pl.program_id(1)

    @pl.when(j == 0)
    def _():
        def body(r, c):
            sl = pl.ds(pl.multiple_of(r * rows, rows), rows)
            xn_ref[sl, :] = _rmsnorm_rows(x_ref[sl, :], g_ref[...]).astype(xn_ref.dtype)
            return c

        lax.fori_loop(0, x_ref.shape[0] // rows, body, 0)

    def proj(w):
        w = w.astype(BF16)
        if w_transposed:
            return lax.dot_general(xn_ref[...], w, (((1,), (1,)), ((), ())), preferred_element_type=F32)
        return _bdot(xn_ref[...], w)

    @pl.when(j < n_main)
    def _():
        o_ref[...] = proj(w_ref[...]).astype(o_ref.dtype)

    if has_tail:
        @pl.when(j == n_main)
        def _():
            ot_ref[...] = proj(wt_ref[...])


def _norm_matmul(x, g, w_stack, layer, n_cols, tm, tn, w_tail=None, w_transposed=False):
    m, k = x.shape
    n_main = n_cols // tn
    has_tail = w_tail is not None
    last = n_main - 1
    if w_transposed:
        w_spec = pl.BlockSpec((None, tn, k), lambda i, j: (layer, jnp.minimum(j, last), 0))
    else:
        w_spec = pl.BlockSpec((None, k, tn), lambda i, j: (layer, 0, jnp.minimum(j, last)))
    in_specs = [
        pl.BlockSpec((tm, k), lambda i, j: (i, 0)),
        pl.BlockSpec((1, k), lambda i, j: (0, 0)),
        w_spec,
    ]
    out_specs = [pl.BlockSpec((tm, tn), lambda i, j: (i, jnp.minimum(j, last)))]
    out_shape = [jax.ShapeDtypeStruct((m, n_cols), BF16)]
    args = [x, g.reshape(1, k), w_stack]
    if has_tail:
        nt = w_tail.shape[1 if w_transposed else 2]
        tail_block = (None, nt, k) if w_transposed else (None, k, nt)
        in_specs.append(pl.BlockSpec(tail_block, lambda i, j: (layer, 0, 0), pipeline_mode=pl.Buffered(1)))
        out_specs.append(pl.BlockSpec((tm, nt), lambda i, j: (i, 0)))
        out_shape.append(jax.ShapeDtypeStruct((m, nt), F32))
        args.append(w_tail)
    outs = pl.pallas_call(
        functools.partial(_norm_matmul_kernel, rows=min(NORM_ROWS, tm), n_main=n_main, has_tail=has_tail,
                          w_transposed=w_transposed),
        grid=(m // tm, n_main + int(has_tail)),
        in_specs=in_specs,
        out_specs=out_specs,
        out_shape=out_shape,
        scratch_shapes=[pltpu.VMEM((tm, k), BF16)],
        compiler_params=_cparams(("parallel", "arbitrary")),
        name="norm_matmul",
    )(*args)
    return outs if has_tail else outs[0]


def _out_proj_kernel(h_ref, mix_ref, mo_ref, wa_ref, wb_ref, o_ref, wa_sc, wb_sc):
    @pl.when(pl.program_id(0) == 0)
    def _():
        wa_sc[...] = wa_ref[...].astype(BF16)
        wb_sc[...] = wb_ref[...].astype(BF16)

    acc = _bdot(mix_ref[...], wa_sc[...])
    acc = acc + _bdot(mo_ref[...], wb_sc[...])
    o_ref[...] = h_ref[...] + acc


def _out_proj(h, mix, mo, w_stack, layer, tm):
    m, d = h.shape
    return pl.pallas_call(
        _out_proj_kernel,
        grid=(m // tm,),
        in_specs=[
            pl.BlockSpec((tm, d), lambda i: (i, 0)),
            pl.BlockSpec((tm, MIX_W), lambda i: (i, 0)),
            pl.BlockSpec((tm, MEM_W), lambda i: (i, 0)),
            pl.BlockSpec((None, MIX_W, d), lambda i: (layer, 0, 0), pipeline_mode=pl.Buffered(1)),
            pl.BlockSpec((None, MEM_W, d), lambda i: (layer, MIX_W // MEM_W, 0), pipeline_mode=pl.Buffered(1)),
        ],
        out_specs=pl.BlockSpec((tm, d), lambda i: (i, 0)),
        out_shape=jax.ShapeDtypeStruct((m, d), F32),
        scratch_shapes=[pltpu.VMEM((MIX_W, d), BF16), pltpu.VMEM((MEM_W, d), BF16)],
        compiler_params=_cparams(("arbitrary",)),
        name="out_proj",
    )(h, mix, mo, w_stack, w_stack)


def _ffn_kernel(h_ref, g_ref, w1_ref, w2_ref, fg_ref, o_ref, xn_ref, *, rows, final_norm):
    j = pl.program_id(1)
    n_rows = h_ref.shape[0] // rows

    @pl.when(j == 0)
    def _():
        def body(r, c):
            sl = pl.ds(pl.multiple_of(r * rows, rows), rows)
            x = h_ref[sl, :]
            xn_ref[sl, :] = _rmsnorm_rows(x, g_ref[...]).astype(xn_ref.dtype)
            o_ref[sl, :] = x
            return c

        lax.fori_loop(0, n_rows, body, 0)

    a = _bdot(xn_ref[...], w1_ref[...].astype(BF16))
    a = jnp.square(jnp.maximum(a, 0.0)).astype(BF16)
    o_ref[...] += _bdot(a, w2_ref[...].astype(BF16))

    if final_norm:
        @pl.when(j == pl.num_programs(1) - 1)
        def _():
            def body(r, c):
                sl = pl.ds(pl.multiple_of(r * rows, rows), rows)
                o_ref[sl, :] = _rmsnorm_rows(o_ref[sl, :], fg_ref[...])
                return c

            lax.fori_loop(0, n_rows, body, 0)


def _ffn(h, g, w1_stack, w2_stack, layer, fg, final_norm, tm, tf):
    m, d = h.shape
    f = w1_stack.shape[2]
    return pl.pallas_call(
        functools.partial(_ffn_kernel, rows=min(NORM_ROWS, tm), final_norm=final_norm),
        grid=(m // tm, f // tf),
        in_specs=[
            pl.BlockSpec((tm, d), lambda i, j: (i, 0)),
            pl.BlockSpec((1, d), lambda i, j: (0, 0)),
            pl.BlockSpec((None, d, tf), lambda i, j: (layer, 0, j)),
            pl.BlockSpec((None, tf, d), lambda i, j: (layer, j, 0)),
            pl.BlockSpec((1, d), lambda i, j: (0, 0)),
        ],
        out_specs=pl.BlockSpec((tm, d), lambda i, j: (i, 0)),
        out_shape=jax.ShapeDtypeStruct((m, d), F32),
        scratch_shapes=[pltpu.VMEM((tm, d), BF16)],
        compiler_params=_cparams(("parallel", "arbitrary")),
        name="ffn",
    )(h, g.reshape(1, d), w1_stack, w2_stack, fg.reshape(1, d))


def _mem_attn_kernel(q_ref, kv_ref, o_ref):
    scale = HEAD_DIM ** -0.5
    for hh in range(N_MEM_HEADS):
        lo, hi = hh * HEAD_DIM, (hh + 1) * HEAD_DIM
        q = q_ref[:, lo:hi].astype(BF16)
        k = kv_ref[:, lo:hi]
        v = kv_ref[:, MEM_W + lo:MEM_W + hi]
        s = lax.dot_general(q, k, (((1,), (1,)), ((), ())), preferred_element_type=F32) * scale
        s = s - jnp.max(s, axis=-1, keepdims=True)
        p = jnp.exp(s)
        l = jnp.sum(p, axis=-1, keepdims=True)
        o = _bdot(p.astype(BF16), v) / l
        o_ref[:, lo:hi] = o.astype(o_ref.dtype)


def _mem_attn(proj, kv, batch, seq, q_block, tq):
    n_q = seq // tq
    return pl.pallas_call(
        _mem_attn_kernel,
        grid=(batch, n_q),
        in_specs=[
            pl.BlockSpec((tq, MEM_W), lambda b, i: (b * n_q + i, q_block)),
            pl.BlockSpec((N_MEM, 2 * MEM_W), lambda b, i: (b, 0)),
        ],
        out_specs=pl.BlockSpec((tq, MEM_W), lambda b, i: (b * n_q + i, 0)),
        out_shape=jax.ShapeDtypeStruct((batch * seq, MEM_W), BF16),
        compiler_params=_cparams(("parallel", "arbitrary")),
        name="mem_attn",
    )(proj, kv)


def _gm_kernel(u_ref, v_ref, vg_ref, ws_ref, bs_ref, o_ref, *, n_chunks):
    v = jax.nn.gelu(v_ref[...].astype(F32))
    vn = _rmsnorm_rows(v, vg_ref[...]).astype(BF16)
    for g in range(GM_GROUPS):
        lo, hi = g * HEAD_DIM, (g + 1) * HEAD_DIM
        rhs = jnp.concatenate(
            [vn[c * GM_CHUNK:(c + 1) * GM_CHUNK, lo:hi] for c in range(n_chunks)], axis=1)
        sg = _bdot(ws_ref[g].astype(BF16), rhs) + bs_ref[:, g:g + 1]
        for c in range(n_chunks):
            rows = slice(c * GM_CHUNK, (c + 1) * GM_CHUNK)
            u = jax.nn.gelu(u_ref[rows, lo:hi].astype(F32))
            o_ref[rows, lo:hi] = (u * sg[:, c * HEAD_DIM:(c + 1) * HEAD_DIM]).astype(o_ref.dtype)


def _gm_mixer(proj, v_g, ws_stack, layer, b_s_t, ts):
    t = proj.shape[0]
    return pl.pallas_call(
        functools.partial(_gm_kernel, n_chunks=ts // GM_CHUNK),
        grid=(t // ts,),
        in_specs=[
            pl.BlockSpec((ts, MIX_W), lambda i: (i, 0)),
            pl.BlockSpec((ts, MIX_W), lambda i: (i, 1)),
            pl.BlockSpec((1, MIX_W), lambda i: (0, 0)),
            pl.BlockSpec((None, GM_GROUPS, GM_CHUNK, GM_CHUNK), lambda i: (layer, 0, 0, 0)),
            pl.BlockSpec((GM_CHUNK, GM_GROUPS), lambda i: (0, 0)),
        ],
        out_specs=pl.BlockSpec((ts, MIX_W), lambda i: (i, 0)),
        out_shape=jax.ShapeDtypeStruct((t, MIX_W), BF16),
        compiler_params=_cparams(("parallel",)),
        name="gm_mixer",
    )(proj, proj, v_g.reshape(1, MIX_W), ws_stack, b_s_t)


def _softplus(x):
    return jnp.maximum(x, 0.0) + jnp.log1p(jnp.exp(-jnp.abs(x)))


def _split3(x):
    hi = x.astype(BF16)
    r1 = x - hi.astype(F32)
    mid = r1.astype(BF16)
    lo = (r1 - mid.astype(F32)).astype(BF16)
    return hi, mid, lo


def _conv_silu(x_ref, cw_ref, shift_ref, m, n_blocks):
    r = DN_ROWS
    seq = x_ref.shape[0]
    row0 = pl.multiple_of(m * r, r)
    main = x_ref[pl.ds(row0, r), :]
    p0 = pl.multiple_of(jnp.maximum(row0 - CONV_HALO, 0), CONV_HALO)
    n0 = pl.multiple_of(jnp.minimum(row0 + r, seq - CONV_HALO), CONV_HALO)
    zero = jnp.zeros((CONV_HALO, x_ref.shape[1]), x_ref.dtype)
    prev = jnp.where(m > 0, x_ref[pl.ds(p0, CONV_HALO), :], zero)
    nxt = jnp.where(m < n_blocks - 1, x_ref[pl.ds(n0, CONV_HALO), :], zero)
    ext = jnp.concatenate([prev, main, nxt], axis=0)
    centre = CONV_W // 2
    h2 = r // 2
    halves = []
    for row in (0, h2):
        shifted = _bdot(shift_ref[...], ext[row:row + h2 + 2 * CONV_HALO, :])
        acc = main[row:row + h2, :].astype(F32) * cw_ref[centre:centre + 1, :]
        for t, j in enumerate(j for j in range(CONV_W) if j != centre):
            acc = acc + shifted[t * h2:(t + 1) * h2, :] * cw_ref[j:j + 1, :]
        halves.append(acc)
    acc = jnp.concatenate(halves, axis=0)
    return acc * jax.nn.sigmoid(acc)


def _dn_kernel(q_ref, k_ref, v_ref, z_ref, gate_ref, cwq_ref, cwk_ref, cwv_ref, alog_ref, dtb_ref,
               ong_ref, o_ref, qs_sc, ks_sc, vs_sc, oacc_sc, nmask_sc, tri_sc, bmask_sc, blk_sc, pair_sc, shift_sc, *,
               n_blocks, hb):
    r = DN_ROWS
    hd = HEAD_DIM
    h2 = r // 2
    head0 = pl.program_id(1) * hb
    lanes = [slice(hh * hd, (hh + 1) * hd) for hh in range(hb)]
    halves = (slice(0, h2), slice(h2, r))

    row_i = lax.broadcasted_iota(jnp.int32, (r, r), 0)
    col_j = lax.broadcasted_iota(jnp.int32, (r, r), 1)
    nmask_sc[0] = jnp.where(col_j <= row_i, 0.0, MASK_OFF)
    nmask_sc[1] = jnp.where(col_j >= row_i, 0.0, MASK_OFF)
    tri_sc[...] = jnp.where(col_j <= row_i, 1.0, 0.0).astype(BF16)
    row_h = lax.broadcasted_iota(jnp.int32, (h2, h2), 0)
    col_h = lax.broadcasted_iota(jnp.int32, (h2, h2), 1)
    diag = row_h == col_h
    same = lambda s: (row_h // s) == (col_h // s)
    bmask_sc[0] = jnp.where(same(DN_BASE), jnp.where(diag, 0.0, -1.0), 0.0).astype(BF16)
    for lvl, s in enumerate(DN_MERGE_SIZES):
        bmask_sc[1 + lvl] = jnp.where(same(2 * s), jnp.where(same(s), 0.0, 1.0), 0.0).astype(BF16)
        blk_sc[lvl] = jnp.where(same(s), 1.0, 0.0).astype(BF16)
        pair_sc[lvl] = jnp.where((row_h // s) % 2 == (col_h // s) % 2, 1.0, 0.0)
    pair_sc[len(DN_MERGE_SIZES)] = jnp.where(col_h % DN_BASE == row_h, 1.0, 0.0)

    n_taps = CONV_W - 1
    out_row = lax.broadcasted_iota(jnp.int32, (n_taps * h2, h2 + 2 * CONV_HALO), 0)
    src_row = lax.broadcasted_iota(jnp.int32, (n_taps * h2, h2 + 2 * CONV_HALO), 1)
    tap = out_row // h2
    offset = tap - CONV_W // 2 + jnp.where(tap >= CONV_W // 2, 1, 0)
    shift_sc[...] = jnp.where(src_row == out_row - tap * h2 + CONV_HALO + offset, 1.0, 0.0).astype(BF16)

    lane = lax.broadcasted_iota(jnp.int32, (1, hd), 1)
    sub = lax.broadcasted_iota(jnp.int32, (hd, 1), 0)

    def pick_col(x, idx):
        return jnp.sum(jnp.where(lane == idx, x, 0.0), axis=1, keepdims=True)

    def pick_row(xt, idx):
        return jnp.sum(jnp.where(sub == idx, xt, 0.0), axis=0, keepdims=True)

    def prep(m, carry):
        rows = pl.ds(pl.multiple_of(m * r, r), r)
        q = _conv_silu(q_ref, cwq_ref, shift_sc, m, n_blocks)
        k = _conv_silu(k_ref, cwk_ref, shift_sc, m, n_blocks)
        v = _conv_silu(v_ref, cwv_ref, shift_sc, m, n_blocks)
        for ln in lanes:
            qh, kh = q[:, ln], k[:, ln]
            qh = qh * lax.rsqrt(jnp.sum(qh * qh, axis=-1, keepdims=True) + L2_EPS) * (hd ** -0.5)
            kh = kh * lax.rsqrt(jnp.sum(kh * kh, axis=-1, keepdims=True) + L2_EPS)
            qs_sc[rows, ln] = qh.astype(BF16)
            ks_sc[rows, ln] = kh.astype(BF16)
        vs_sc[rows, :] = v.astype(BF16)
        return carry

    lax.fori_loop(0, n_blocks, prep, 0)
    oacc_sc[...] = jnp.zeros_like(oacc_sc)

    def gate_block(m):
        rows = pl.ds(pl.multiple_of(m * r, r), r)
        graw = gate_ref[rows, :]
        gval = -jnp.exp(alog_ref[...]) * _softplus(graw + dtb_ref[...])
        bval = jax.nn.sigmoid(graw)
        hi, mid, lo = _split3(gval)
        cs = _bdot(tri_sc[...], jnp.concatenate([hi, mid, lo], axis=1))
        prefix = cs[:, :hd] + cs[:, hd:2 * hd] + cs[:, 2 * hd:]
        return gval, bval, prefix

    units = [(hh, d) for hh in range(hb) for d in range(2)]
    nu = len(units)
    hidx = [(u, hf) for u in range(nu) for hf in range(2)]
    order = [(0, 1) if d == 0 else (1, 0) for _, d in units]
    top = len(DN_MERGE_SIZES) - 1

    def strip(x, s):
        acc = x[0:s]
        for g in range(1, h2 // s):
            acc = acc + x[g * s:(g + 1) * s]
        return acc

    def expand(xs, lvl):
        return jnp.concatenate([xs] * (h2 // xs.shape[0]), axis=0) * blk_sc[lvl]

    def widen(xs, lvl):
        return jnp.concatenate([xs, xs], axis=0) * pair_sc[lvl, 0:2 * xs.shape[0], :]

    def block_rows(n):
        n = jnp.clip(n, 0, n_blocks - 1)
        blocks = (n, n_blocks - 1 - n)
        return blocks, [pl.ds(pl.multiple_of(blocks[d] * r, r), r) for _, d in units]

    def stage_a(n):
        blocks, rows = block_rows(n)
        gval_f, bval_f, gc_f = gate_block(blocks[0])
        gval_b, bval_b, pre_b = gate_block(blocks[1])
        yield
        gc_b = pre_b[r - 1:r, :] - pre_b + gval_b
        gcs = (gc_f, gc_b)
        gcts = (gc_f.T, gc_b.T)
        bvals = (bval_f, bval_b)
        gc_col, gc_row, beta, g_last, q_b, k_b, q, k, v, kbeta = ([] for _ in range(10))
        for (hh, d), rw in zip(units, rows):
            lg = d * DN_HEADS + head0 + hh
            gc_col.append(pick_col(gcs[d], lg))
            gc_row.append(pick_row(gcts[d], lg))
            beta.append(pick_col(bvals[d], 2 * DN_HEADS + lg))
            g_last.append(gc_col[-1][r - 1:r, :] if d == 0 else gc_col[-1][0:1, :])
            q_b.append(qs_sc[rw, lanes[hh]])
            k_b.append(ks_sc[rw, lanes[hh]])
            q.append(q_b[-1].astype(F32))
            k.append(k_b[-1].astype(F32))
            v.append(vs_sc[rw, lanes[hh]].astype(F32))
            kbeta.append(k[-1] * beta[-1])
        qkk = [lax.dot_general(jnp.concatenate([q_b[u], kbeta[u].astype(BF16)], axis=0), k_b[u],
                               (((1,), (1,)), ((), ())), preferred_element_type=F32) for u in range(nu)]
        yield
        dec = [jnp.exp(gc_col[u] - gc_row[u] + nmask_sc[units[u][1]]) for u in range(nu)]
        attn = [(qkk[u][:r] * dec[u]).astype(BF16) for u in range(nu)]
        a_b = [(qkk[u][r:] * dec[u]).astype(BF16) for u in range(nu)]
        e_gc = [jnp.exp(gc_col[u]) for u in range(nu)]
        rhs = [jnp.concatenate([v[u] * beta[u], kbeta[u] * e_gc[u]], axis=1).astype(BF16) for u in range(nu)]
        qd = [(q[u] * e_gc[u]).astype(BF16) for u in range(nu)]
        kdt = [(k[u] * jnp.exp(g_last[u] - gc_col[u])).T.astype(BF16) for u in range(nu)]
        cd = [jnp.broadcast_to(jnp.exp(g), (8, hd)) for g in g_last]

        a_h = [a_b[u][halves[hf], halves[hf]] for u, hf in hidx]
        a_x = [a_b[u][halves[se], halves[fi]] for u, (fi, se) in enumerate(order)]
        y_f = [a * bmask_sc[0] for a in a_h]
        y_s = [strip(yy, DN_BASE) for yy in y_f]
        t_s = [ys.astype(F32) + pair_sc[len(DN_MERGE_SIZES), 0:DN_BASE, :] for ys in y_s]
        y_s = [_bdot(ys, yf).astype(BF16) for ys, yf in zip(y_s, y_f)]
        yield
        y_f = [expand(ys, 0) for ys in y_s]
        for _ in range(DN_BASE.bit_length() - 3):
            st = [_bdot(jnp.concatenate([ts.astype(BF16), ys], axis=0), yf) for ts, ys, yf in zip(t_s, y_s, y_f)]
            yield
            t_s = [ts + x[:DN_BASE] for ts, x in zip(t_s, st)]
            y_s = [x[DN_BASE:].astype(BF16) for x in st]
            y_f = [expand(ys, 0) for ys in y_s]
        t_s = [ts + _bdot(ts.astype(BF16), yf) for ts, yf in zip(t_s, y_f)]
        yield
        t_s = yield from merge_strips(t_s, a_h, 0)
        return tuple(t_s), tuple(a_h), tuple(a_x), tuple(attn), tuple(rhs), tuple(qd), tuple(kdt), tuple(cd)

    def merge_strips(t_s, a_h, lvl):
        s = DN_MERGE_SIZES[lvl]
        t_f = [expand(ts.astype(BF16), lvl) for ts in t_s]
        l_s = [strip(a * bmask_sc[1 + lvl], 2 * s) for a in a_h]
        m1 = [_bdot(ls, tf).astype(BF16) for ls, tf in zip(l_s, t_f)]
        yield
        m1_f = [expand(mm, lvl + 1) for mm in m1]
        t_s = [widen(ts, lvl) for ts in t_s]
        t_s = [ts - _bdot(ts.astype(BF16), mf) for ts, mf in zip(t_s, m1_f)]
        yield
        return t_s

    def stage_b(ctx):
        t_s, a_h, a_x, attn, rhs, qd, kdt, cd = ctx
        t_s = list(t_s)
        for lvl in range(1, top):
            t_s = yield from merge_strips(t_s, a_h, lvl)
        t = [widen(ts, top) for ts in t_s]
        t_b = [tt.astype(BF16) for tt in t]
        m1 = [_bdot(a * bmask_sc[1 + top], tb).astype(BF16) for a, tb in zip(a_h, t_b)]
        yield
        t = [tt - _bdot(tb, mm) for tt, tb, mm in zip(t, t_b, m1)]
        yield
        t_b = [tt.astype(BF16) for tt in t]
        p = [_bdot(a_x[u], t_b[2 * u + fi]).astype(BF16) for u, (fi, se) in enumerate(order)]
        yield
        x_b = [(-_bdot(t_b[2 * u + se], p[u])).astype(BF16) for u, (fi, se) in enumerate(order)]
        yield
        zero = jnp.zeros((h2, h2), BF16)
        t_full = []
        for u, (_, d) in enumerate(units):
            if d == 0:
                upper = jnp.concatenate([t_b[2 * u], zero], axis=1)
                lower = jnp.concatenate([x_b[u], t_b[2 * u + 1]], axis=1)
            else:
                upper = jnp.concatenate([t_b[2 * u], x_b[u]], axis=1)
                lower = jnp.concatenate([zero, t_b[2 * u + 1]], axis=1)
            t_full.append(jnp.concatenate([upper, lower], axis=0))
        uw = [_bdot(t_full[u], rhs[u]) for u in range(nu)]
        yield
        u_f = tuple(x[:, :hd] for x in uw)
        wq = tuple(jnp.concatenate([uw[u][:, hd:].astype(BF16), qd[u]], axis=0) for u in range(nu))
        return u_f, wq, attn, kdt, cd

    def stage_c(ctx, states, n):
        u_f, wq, attn, kdt, cd = ctx
        _, rows = block_rows(n)
        r1 = [_bdot(wq[u], states[u].astype(BF16)) for u in range(nu)]
        yield
        v_new = [(u_f[u] - r1[u][:r]).astype(BF16) for u in range(nu)]
        r2 = [_bdot(jnp.concatenate([attn[u], kdt[u]], axis=0), v_new[u]) for u in range(nu)]
        yield
        for u, (hh, _) in enumerate(units):
            oacc_sc[rows[u], lanes[hh]] += r1[u][r:] + r2[u][:r]
        return tuple(states[u] * cd[u][0:1, :] + r2[u][r:] for u in range(nu))

    def interleave(gens):
        results = [None] * len(gens)
        live = list(range(len(gens)))
        while live:
            for i in list(live):
                try:
                    next(gens[i])
                except StopIteration as stop:
                    results[i] = stop.value
                    live.remove(i)
        return results

    def scan(i, states):
        ns = [i * DN_BLOCKS_PER_ITER + b for b in range(DN_BLOCKS_PER_ITER)]
        ctx = interleave([stage_a(n) for n in ns])
        ctx = interleave([stage_b(c) for c in ctx])
        for c, n in zip(ctx, ns):
            (states,) = interleave([stage_c(c, states, n)])
        return states

    lax.fori_loop(0, n_blocks // DN_BLOCKS_PER_ITER, scan, (jnp.zeros((hd, hd), F32),) * nu)

    def finish(m, carry):
        rows = pl.ds(pl.multiple_of(m * r, r), r)
        for ln in lanes:
            o = oacc_sc[rows, ln]
            ms = jnp.mean(o * o, axis=-1, keepdims=True)
            o = o * lax.rsqrt(ms + RMS_EPS) * ong_ref[...]
            z = z_ref[rows, ln].astype(F32)
            o_ref[rows, ln] = (o * (z * jax.nn.sigmoid(z))).astype(o_ref.dtype)
        return carry

    lax.fori_loop(0, n_blocks, finish, 0)


def _dn_mixer(proj, tail, conv_w_stack, layer, a_log, dt_bias, out_g, batch, seq):
    r = DN_ROWS
    hb = DN_HEADS_PER_STEP
    n_blocks = seq // r
    n_groups = DN_HEADS // hb
    width = hb * HEAD_DIM
    proj3 = proj.reshape(batch, seq, proj.shape[1])
    tail3 = tail.reshape(batch, seq, DN_TAIL_W)
    pad = HEAD_DIM - 2 * DN_HEADS
    alog = jnp.pad(a_log.reshape(1, 2 * DN_HEADS).astype(F32), ((0, 0), (0, pad)))
    dtb = jnp.pad(dt_bias.reshape(1, 2 * DN_HEADS).astype(F32), ((0, 0), (0, pad)))
    def col(off, buffers):
        return pl.BlockSpec((None, seq, width), lambda b, g: (b, 0, off * n_groups + g),
                            pipeline_mode=pl.Buffered(buffers))
    cw = lambda off: pl.BlockSpec((None, CONV_W, width), lambda b, g: (layer, 0, off * n_groups + g))
    row = pl.BlockSpec((1, HEAD_DIM), lambda b, g: (0, 0))
    out = pl.pallas_call(
        functools.partial(_dn_kernel, n_blocks=n_blocks, hb=hb),
        grid=(batch, n_groups),
        in_specs=[
            col(0, 2), col(1, 1), col(2, 1), col(3, 1),
            pl.BlockSpec((None, seq, HEAD_DIM), lambda b, g: (b, 0, MEM_W // HEAD_DIM)),
            cw(0), cw(1), cw(2),
            row, row, row,
        ],
        out_specs=pl.BlockSpec((None, seq, width), lambda b, g: (b, 0, g)),
        out_shape=jax.ShapeDtypeStruct((batch, seq, MIX_W), BF16),
        scratch_shapes=[
            pltpu.VMEM((seq, width), BF16),
            pltpu.VMEM((seq, width), BF16),
            pltpu.VMEM((seq, width), BF16),
            pltpu.VMEM((seq, width), F32),
            pltpu.VMEM((2, r, r), F32),
            pltpu.VMEM((r, r), BF16),
            pltpu.VMEM((1 + len(DN_MERGE_SIZES), r // 2, r // 2), BF16),
            pltpu.VMEM((len(DN_MERGE_SIZES), r // 2, r // 2), BF16),
            pltpu.VMEM((1 + len(DN_MERGE_SIZES), r // 2, r // 2), F32),
            pltpu.VMEM(((CONV_W - 1) * (r // 2), r // 2 + 2 * CONV_HALO), BF16),
        ],
        compiler_params=_cparams(("parallel", "arbitrary")),
        name="dn_mixer",
    )(proj3, proj3, proj3, proj3, tail3, conv_w_stack, conv_w_stack, conv_w_stack, alog, dtb,
      out_g.reshape(1, HEAD_DIM).astype(F32))
    return out.reshape(batch * seq, MIX_W)


def kernel(x, mem, mem_norm_g, tok_norm_g, dn_w_in, dn_conv_w, dn_a_log, dn_dt_bias, dn_out_norm_g,
           gm_w_in, gm_v_norm_g, gm_w_s, gm_b_s, mem_w_kv, tok_w_out, ffn_norm_g, ffn_w1, ffn_w2,
           final_norm_g):
    batch, seq, d = x.shape
    t = batch * seq
    h = x.reshape(t, d)
    mem2 = mem.reshape(batch * N_MEM, d)
    dn_w_t = jnp.swapaxes(dn_w_in, 1, 2)
    dn_w_tail = jnp.concatenate(
        [dn_w_t[:, 4 * MIX_W + DN_GATES:], dn_w_t[:, 4 * MIX_W:4 * MIX_W + DN_GATES],
         jnp.zeros((dn_w_t.shape[0], HEAD_DIM - DN_GATES, d), F32)], axis=1)
    for layer in range(DEPTH):
        j = layer // 2
        g_tok = tok_norm_g[layer]
        if layer % 2 == 0:
            proj, tail = _norm_matmul(h, g_tok, dn_w_t, j, 4 * MIX_W, 1024, 768, dn_w_tail, w_transposed=True)
            mix = _dn_mixer(proj, tail, dn_conv_w, j, dn_a_log[j], dn_dt_bias[j], dn_out_norm_g[j],
                            batch, seq)
            q_src, q_block = tail, 0
        else:
            proj = _norm_matmul(h, g_tok, gm_w_in, j, 2 * MIX_W + MEM_W, 1024, 896)
            mix = _gm_mixer(proj, gm_v_norm_g[j], gm_w_s, j, gm_b_s[j].T.astype(F32), 512)
            q_src, q_block = proj, 2 * MIX_W // MEM_W
        kv = _norm_matmul(mem2, mem_norm_g, mem_w_kv, layer, 2 * MEM_W, batch * N_MEM, 512)
        mo = _mem_attn(q_src, kv, batch, seq, q_block, min(1024, seq))
        h = _out_proj(h, mix, mo, tok_w_out, layer, 512)
        h = _ffn(h, ffn_norm_g[layer], ffn_w1, ffn_w2, layer, final_norm_g, layer == DEPTH - 1, 1024, 512)
    return h.reshape(batch, seq, d)
```

```python
import functools

import jax
import jax.numpy as jnp
from jax import lax
from jax.experimental import pallas as pl
from jax.experimental.pallas import tpu as pltpu

D_MODEL = 2048
DEPTH = 4
N_MEM = 256
HEAD_DIM = 128
N_MEM_HEADS = 4
MEM_W = N_MEM_HEADS * HEAD_DIM
MIX_W = D_MODEL - MEM_W
DN_HEADS = MIX_W // HEAD_DIM
DN_CHUNK = 64
CONV_W = 5
GM_GROUPS = MIX_W // HEAD_DIM
GM_CHUNK = 128
FFN_HIDDEN = 4 * D_MODEL
RMS_EPS = 1e-6
L2_EPS = 1e-6

F32 = jnp.float32
BF16 = jnp.bfloat16

DN_ROWS = 256
DN_BASE = 16
DN_MERGE_SIZES = (16, 32, 64)
DN_BLOCKS_PER_ITER = 2
DN_HEADS_PER_STEP = 3
DN_PREP_LANES = 512
CONV_HALO = 16
NORM_ROWS = 256
DN_GATES = 4 * DN_HEADS
DN_TAIL_W = MEM_W + HEAD_DIM
MASK_OFF = -1e30
VMEM_LIMIT = 56 * 1024 * 1024


def _cparams(semantics):
    return pltpu.CompilerParams(dimension_semantics=semantics, vmem_limit_bytes=VMEM_LIMIT)


def _rmsnorm_rows(x, g):
    ms = jnp.mean(x * x, axis=-1, keepdims=True)
    return x * lax.rsqrt(ms + RMS_EPS) * g


def _bdot(a, b):
    return jnp.dot(a, b, preferred_element_type=F32)


def _norm_matmul_kernel(x_ref, g_ref, w_ref, *rest, rows, n_main, has_tail, w_transposed):
    if has_tail:
        wt_ref, o_ref, ot_ref, xn_ref = rest
    else:
        o_ref, xn_ref = rest
    j = pl.program_id(1)

    @pl.when(j == 0)
    def _():
        def body(r, c):
            sl = pl.ds(pl.multiple_of(r * rows, rows), rows)
            xn_ref[sl, :] = _rmsnorm_rows(x_ref[sl, :], g_ref[...]).astype(xn_ref.dtype)
            return c

        lax.fori_loop(0, x_ref.shape[0] // rows, body, 0)

    def proj(w):
        w = w.astype(BF16)
        if w_transposed:
            return lax.dot_general(xn_ref[...], w, (((1,), (1,)), ((), ())), preferred_element_type=F32)
        return _bdot(xn_ref[...], w)

    @pl.when(j < n_main)
    def _():
        o_ref[...] = proj(w_ref[...]).astype(o_ref.dtype)

    if has_tail:
        @pl.when(j == n_main)
        def _():
            ot_ref[...] = proj(wt_ref[...])


def _norm_matmul(x, g, w_stack, layer, n_cols, tm, tn, w_tail=None, w_transposed=False):
    m, k = x.shape
    n_main = n_cols // tn
    has_tail = w_tail is not None
    last = n_main - 1
    if w_transposed:
        w_spec = pl.BlockSpec((None, tn, k), lambda i, j: (layer, jnp.minimum(j, last), 0))
    else:
        w_spec = pl.BlockSpec((None, k, tn), lambda i, j: (layer, 0, jnp.minimum(j, last)))
    in_specs = [
        pl.BlockSpec((tm, k), lambda i, j: (i, 0)),
        pl.BlockSpec((1, k), lambda i, j: (0, 0)),
        w_spec,
    ]
    out_specs = [pl.BlockSpec((tm, tn), lambda i, j: (i, jnp.minimum(j, last)))]
    out_shape = [jax.ShapeDtypeStruct((m, n_cols), BF16)]
    args = [x, g.reshape(1, k), w_stack]
    if has_tail:
        nt = w_tail.shape[1 if w_transposed else 2]
        tail_block = (None, nt, k) if w_transposed else (None, k, nt)
        in_specs.append(pl.BlockSpec(tail_block, lambda i, j: (layer, 0, 0), pipeline_mode=pl.Buffered(1)))
        out_specs.append(pl.BlockSpec((tm, nt), lambda i, j: (i, 0)))
        out_shape.append(jax.ShapeDtypeStruct((m, nt), F32))
        args.append(w_tail)
    outs = pl.pallas_call(
        functools.partial(_norm_matmul_kernel, rows=min(NORM_ROWS, tm), n_main=n_main, has_tail=has_tail,
                          w_transposed=w_transposed),
        grid=(m // tm, n_main + int(has_tail)),
        in_specs=in_specs,
        out_specs=out_specs,
        out_shape=out_shape,
        scratch_shapes=[pltpu.VMEM((tm, k), BF16)],
        compiler_params=_cparams(("parallel", "arbitrary")),
        name="norm_matmul",
    )(*args)
    return outs if has_tail else outs[0]


def _out_proj_kernel(h_ref, mix_ref, mo_ref, wa_ref, wb_ref, o_ref, wa_sc, wb_sc):
    @pl.when(pl.program_id(0) == 0)
    def _():
        wa_sc[...] = wa_ref[...].astype(BF16)
        wb_sc[...] = wb_ref[...].astype(BF16)

    acc = _bdot(mix_ref[...], wa_sc[...])
    acc = acc + _bdot(mo_ref[...], wb_sc[...])
    o_ref[...] = h_ref[...] + acc


def _out_proj(h, mix, mo, w_stack, layer, tm):
    m, d = h.shape
    return pl.pallas_call(
        _out_proj_kernel,
        grid=(m // tm,),
        in_specs=[
            pl.BlockSpec((tm, d), lambda i: (i, 0)),
            pl.BlockSpec((tm, MIX_W), lambda i: (i, 0)),
            pl.BlockSpec((tm, MEM_W), lambda i: (i, 0)),
            pl.BlockSpec((None, MIX_W, d), lambda i: (layer, 0, 0), pipeline_mode=pl.Buffered(1)),
            pl.BlockSpec((None, MEM_W, d), lambda i: (layer, MIX_W // MEM_W, 0), pipeline_mode=pl.Buffered(1)),
        ],
        out_specs=pl.BlockSpec((tm, d), lambda i: (i, 0)),
        out_shape=jax.ShapeDtypeStruct((m, d), F32),
        scratch_shapes=[pltpu.VMEM((MIX_W, d), BF16), pltpu.VMEM((MEM_W, d), BF16)],
        compiler_params=_cparams(("arbitrary",)),
        name="out_proj",
    )(h, mix, mo, w_stack, w_stack)


def _ffn_kernel(h_ref, g_ref, w1_ref, w2_ref, fg_ref, o_ref, xn_ref, *, rows, final_norm):
    j = pl.program_id(1)
    n_rows = h_ref.shape[0] // rows

    @pl.when(j == 0)
    def _():
        def body(r, c):
            sl = pl.ds(pl.multiple_of(r * rows, rows), rows)
            x = h_ref[sl, :]
            xn_ref[sl, :] = _rmsnorm_rows(x, g_ref[...]).astype(xn_ref.dtype)
            o_ref[sl, :] = x
            return c

        lax.fori_loop(0, n_rows, body, 0)

    a = _bdot(xn_ref[...], w1_ref[...].astype(BF16))
    a = jnp.square(jnp.maximum(a, 0.0)).astype(BF16)
    o_ref[...] += _bdot(a, w2_ref[...].astype(BF16))

    if final_norm:
        @pl.when(j == pl.num_programs(1) - 1)
        def _():
            def body(r, c):
                sl = pl.ds(pl.multiple_of(r * rows, rows), rows)
                o_ref[sl, :] = _rmsnorm_rows(o_ref[sl, :], fg_ref[...])
                return c

            lax.fori_loop(0, n_rows, body, 0)


def _ffn(h, g, w1_stack, w2_stack, layer, fg, final_norm, tm, tf):
    m, d = h.shape
    f = w1_stack.shape[2]
    return pl.pallas_call(
        functools.partial(_ffn_kernel, rows=min(NORM_ROWS, tm), final_norm=final_norm),
        grid=(m // tm, f // tf),
        in_specs=[
            pl.BlockSpec((tm, d), lambda i, j: (i, 0)),
            pl.BlockSpec((1, d), lambda i, j: (0, 0)),
            pl.BlockSpec((None, d, tf), lambda i, j: (layer, 0, j)),
            pl.BlockSpec((None, tf, d), lambda i, j: (layer, j, 0)),
            pl.BlockSpec((1, d), lambda i, j: (0, 0)),
        ],
        out_specs=pl.BlockSpec((tm, d), lambda i, j: (i, 0)),
        out_shape=jax.ShapeDtypeStruct((m, d), F32),
        scratch_shapes=[pltpu.VMEM((tm, d), BF16)],
        compiler_params=_cparams(("parallel", "arbitrary")),
        name="ffn",
    )(h, g.reshape(1, d), w1_stack, w2_stack, fg.reshape(1, d))


def _mem_attn_kernel(q_ref, kv_ref, o_ref):
    scale = HEAD_DIM ** -0.5
    for hh in range(N_MEM_HEADS):
        lo, hi = hh * HEAD_DIM, (hh + 1) * HEAD_DIM
        q = q_ref[:, lo:hi].astype(BF16)
        k = kv_ref[:, lo:hi]
        v = kv_ref[:, MEM_W + lo:MEM_W + hi]
        s = lax.dot_general(q, k, (((1,), (1,)), ((), ())), preferred_element_type=F32) * scale
        s = s - jnp.max(s, axis=-1, keepdims=True)
        p = jnp.exp(s)
        l = jnp.sum(p, axis=-1, keepdims=True)
        o = _bdot(p.astype(BF16), v) / l
        o_ref[:, lo:hi] = o.astype(o_ref.dtype)


def _mem_attn(proj, kv, batch, seq, q_block, tq):
    n_q = seq // tq
    return pl.pallas_call(
        _mem_attn_kernel,
        grid=(batch, n_q),
        in_specs=[
            pl.BlockSpec((tq, MEM_W), lambda b, i: (b * n_q + i, q_block)),
            pl.BlockSpec((N_MEM, 2 * MEM_W), lambda b, i: (b, 0)),
        ],
        out_specs=pl.BlockSpec((tq, MEM_W), lambda b, i: (b * n_q + i, 0)),
        out_shape=jax.ShapeDtypeStruct((batch * seq, MEM_W), BF16),
        compiler_params=_cparams(("parallel", "arbitrary")),
        name="mem_attn",
    )(proj, kv)


def _gm_kernel(u_ref, v_ref, vg_ref, ws_ref, bs_ref, o_ref, *, n_chunks):
    v = jax.nn.gelu(v_ref[...].astype(F32))
    vn = _rmsnorm_rows(v, vg_ref[...]).astype(BF16)
    for g in range(GM_GROUPS):
        lo, hi = g * HEAD_DIM, (g + 1) * HEAD_DIM
        rhs = jnp.concatenate(
            [vn[c * GM_CHUNK:(c + 1) * GM_CHUNK, lo:hi] for c in range(n_chunks)], axis=1)
        sg = _bdot(ws_ref[g].astype(BF16), rhs) + bs_ref[:, g:g + 1]
        for c in range(n_chunks):
            rows = slice(c * GM_CHUNK, (c + 1) * GM_CHUNK)
            u = jax.nn.gelu(u_ref[rows, lo:hi].astype(F32))
            o_ref[rows, lo:hi] = (u * sg[:, c * HEAD_DIM:(c + 1) * HEAD_DIM]).astype(o_ref.dtype)


def _gm_mixer(proj, v_g, ws_stack, layer, b_s_t, ts):
    t = proj.shape[0]
    return pl.pallas_call(
        functools.partial(_gm_kernel, n_chunks=ts // GM_CHUNK),
        grid=(t // ts,),
        in_specs=[
            pl.BlockSpec((ts, MIX_W), lambda i: (i, 0)),
            pl.BlockSpec((ts, MIX_W), lambda i: (i, 1)),
            pl.BlockSpec((1, MIX_W), lambda i: (0, 0)),
            pl.BlockSpec((None, GM_GROUPS, GM_CHUNK, GM_CHUNK), lambda i: (layer, 0, 0, 0)),
            pl.BlockSpec((GM_CHUNK, GM_GROUPS), lambda i: (0, 0)),
        ],
        out_specs=pl.BlockSpec((ts, MIX_W), lambda i: (i, 0)),
        out_shape=jax.ShapeDtypeStruct((t, MIX_W), BF16),
        compiler_params=_cparams(("parallel",)),
        name="gm_mixer",
    )(proj, proj, v_g.reshape(1, MIX_W), ws_stack, b_s_t)


def _softplus(x):
    return jnp.maximum(x, 0.0) + jnp.log1p(jnp.exp(-jnp.abs(x)))


def _split3(x):
    hi = x.astype(BF16)
    r1 = x - hi.astype(F32)
    mid = r1.astype(BF16)
    lo = (r1 - mid.astype(F32)).astype(BF16)
    return hi, mid, lo


def _conv_silu(prev, main, nxt, cw_ref, shift_ref):
    r = DN_ROWS
    ext = jnp.concatenate([prev, main, nxt], axis=0)
    centre = CONV_W // 2
    h2 = r // 2
    halves = []
    for row in (0, h2):
        shifted = _bdot(shift_ref[...], ext[row:row + h2 + 2 * CONV_HALO, :])
        acc = main[row:row + h2, :].astype(F32) * cw_ref[centre:centre + 1, :]
        for t, j in enumerate(j for j in range(CONV_W) if j != centre):
            acc = acc + shifted[t * h2:(t + 1) * h2, :] * cw_ref[j:j + 1, :]
        halves.append(acc)
    acc = jnp.concatenate(halves, axis=0)
    return acc * jax.nn.sigmoid(acc)


def _dn_prep_kernel(qp_ref, qm_ref, qn_ref, kp_ref, km_ref, kn_ref, vp_ref, vm_ref, vn_ref,
                    cwq_ref, cwk_ref, cwv_ref, qo_ref, ko_ref, vo_ref, shift_sc, *, lane_group):
    r = DN_ROWS
    h2 = r // 2
    m = pl.program_id(1)
    last = pl.num_programs(1) - 1
    n_taps = CONV_W - 1
    out_row = lax.broadcasted_iota(jnp.int32, (n_taps * h2, h2 + 2 * CONV_HALO), 0)
    src_row = lax.broadcasted_iota(jnp.int32, (n_taps * h2, h2 + 2 * CONV_HALO), 1)
    tap = out_row // h2
    offset = tap - CONV_W // 2 + jnp.where(tap >= CONV_W // 2, 1, 0)
    shift_sc[...] = jnp.where(src_row == out_row - tap * h2 + CONV_HALO + offset, 1.0, 0.0).astype(BF16)

    streams = ((qp_ref, qm_ref, qn_ref, cwq_ref, qo_ref, HEAD_DIM ** -0.5),
               (kp_ref, km_ref, kn_ref, cwk_ref, ko_ref, 1.0),
               (vp_ref, vm_ref, vn_ref, cwv_ref, vo_ref, None))
    for p_ref, m_ref, n_ref, cw_ref, o_ref, scale in streams:
        for c0 in range(0, m_ref.shape[1], lane_group):
            cols = slice(c0, c0 + lane_group)
            zero = jnp.zeros((CONV_HALO, lane_group), m_ref.dtype)
            prev = jnp.where(m > 0, p_ref[:, cols], zero)
            nxt = jnp.where(m < last, n_ref[:, cols], zero)
            y = _conv_silu(prev, m_ref[:, cols], nxt, cw_ref.at[:, cols], shift_sc)
            for h0 in range(0, lane_group, HEAD_DIM):
                yh = y[:, h0:h0 + HEAD_DIM]
                if scale is not None:
                    yh = yh * lax.rsqrt(jnp.sum(yh * yh, axis=-1, keepdims=True) + L2_EPS) * scale
                o_ref[:, c0 + h0:c0 + h0 + HEAD_DIM] = yh.astype(o_ref.dtype)


def _dn_prep(proj3, conv_w_stack, layer, batch, seq):
    r = DN_ROWS
    n_blocks = seq // r
    per_halo = r // CONV_HALO
    n_halo = seq // CONV_HALO

    def stream(t):
        return [
            pl.BlockSpec((None, CONV_HALO, MIX_W), lambda b, m: (b, jnp.maximum(m * per_halo - 1, 0), t)),
            pl.BlockSpec((None, r, MIX_W), lambda b, m: (b, m, t)),
            pl.BlockSpec((None, CONV_HALO, MIX_W), lambda b, m: (b, jnp.minimum((m + 1) * per_halo, n_halo - 1), t)),
        ]

    cw = lambda t: pl.BlockSpec((None, CONV_W, MIX_W), lambda b, m: (layer, 0, t))
    out_spec = pl.BlockSpec((None, r, MIX_W), lambda b, m: (b, m, 0))
    out_shape = jax.ShapeDtypeStruct((batch, seq, MIX_W), BF16)
    return pl.pallas_call(
        functools.partial(_dn_prep_kernel, lane_group=DN_PREP_LANES),
        grid=(batch, n_blocks),
        in_specs=stream(0) + stream(1) + stream(2) + [cw(0), cw(1), cw(2)],
        out_specs=[out_spec] * 3,
        out_shape=[out_shape] * 3,
        scratch_shapes=[pltpu.VMEM(((CONV_W - 1) * (r // 2), r // 2 + 2 * CONV_HALO), BF16)],
        compiler_params=_cparams(("parallel", "arbitrary")),
        name="dn_prep",
    )(*([proj3] * 9), conv_w_stack, conv_w_stack, conv_w_stack)


def _dn_kernel(q_ref, k_ref, v_ref, z_ref, gate_ref, alog_ref, dtb_ref, ong_ref, o_ref,
               oacc_sc, nmask_sc, tri_sc, bmask_sc, blk_sc, pair_sc, *, n_blocks, hb):
    r = DN_ROWS
    hd = HEAD_DIM
    h2 = r // 2
    head0 = pl.program_id(1) * hb
    lanes = [slice(hh * hd, (hh + 1) * hd) for hh in range(hb)]
    halves = (slice(0, h2), slice(h2, r))

    row_i = lax.broadcasted_iota(jnp.int32, (r, r), 0)
    col_j = lax.broadcasted_iota(jnp.int32, (r, r), 1)
    nmask_sc[0] = jnp.where(col_j <= row_i, 0.0, MASK_OFF)
    nmask_sc[1] = jnp.where(col_j >= row_i, 0.0, MASK_OFF)
    tri_sc[...] = jnp.where(col_j <= row_i, 1.0, 0.0).astype(BF16)
    row_h = lax.broadcasted_iota(jnp.int32, (h2, h2), 0)
    col_h = lax.broadcasted_iota(jnp.int32, (h2, h2), 1)
    diag = row_h == col_h
    same = lambda s: (row_h // s) == (col_h // s)
    bmask_sc[0] = jnp.where(same(DN_BASE), jnp.where(diag, 0.0, -1.0), 0.0).astype(BF16)
    for lvl, s in enumerate(DN_MERGE_SIZES):
        bmask_sc[1 + lvl] = jnp.where(same(2 * s), jnp.where(same(s), 0.0, 1.0), 0.0).astype(BF16)
        blk_sc[lvl] = jnp.where(same(s), 1.0, 0.0).astype(BF16)
        pair_sc[lvl] = jnp.where((row_h // s) % 2 == (col_h // s) % 2, 1.0, 0.0)
    pair_sc[len(DN_MERGE_SIZES)] = jnp.where(col_h % DN_BASE == row_h, 1.0, 0.0)

    lane = lax.broadcasted_iota(jnp.int32, (1, hd), 1)
    sub = lax.broadcasted_iota(jnp.int32, (hd, 1), 0)

    def pick_col(x, idx):
        return jnp.sum(jnp.where(lane == idx, x, 0.0), axis=1, keepdims=True)

    def pick_row(xt, idx):
        return jnp.sum(jnp.where(sub == idx, xt, 0.0), axis=0, keepdims=True)

    oacc_sc[...] = jnp.zeros_like(oacc_sc)

    def gate_block(m):
        rows = pl.ds(pl.multiple_of(m * r, r), r)
        graw = gate_ref[rows, :]
        gval = -jnp.exp(alog_ref[...]) * _softplus(graw + dtb_ref[...])
        bval = jax.nn.sigmoid(graw)
        hi, mid, lo = _split3(gval)
        cs = _bdot(tri_sc[...], jnp.concatenate([hi, mid, lo], axis=1))
        prefix = cs[:, :hd] + cs[:, hd:2 * hd] + cs[:, 2 * hd:]
        return gval, bval, prefix

    units = [(hh, d) for hh in range(hb) for d in range(2)]
    nu = len(units)
    hidx = [(u, hf) for u in range(nu) for hf in range(2)]
    order = [(0, 1) if d == 0 else (1, 0) for _, d in units]
    top = len(DN_MERGE_SIZES) - 1

    def strip(x, s):
        acc = x[0:s]
        for g in range(1, h2 // s):
            acc = acc + x[g * s:(g + 1) * s]
        return acc

    def expand(xs, lvl):
        return jnp.concatenate([xs] * (h2 // xs.shape[0]), axis=0) * blk_sc[lvl]

    def widen(xs, lvl):
        return jnp.concatenate([xs, xs], axis=0) * pair_sc[lvl, 0:2 * xs.shape[0], :]

    def block_rows(n):
        n = jnp.clip(n, 0, n_blocks - 1)
        blocks = (n, n_blocks - 1 - n)
        return blocks, [pl.ds(pl.multiple_of(blocks[d] * r, r), r) for _, d in units]

    def stage_a(n):
        blocks, rows = block_rows(n)
        gval_f, bval_f, gc_f = gate_block(blocks[0])
        gval_b, bval_b, pre_b = gate_block(blocks[1])
        yield
        gc_b = pre_b[r - 1:r, :] - pre_b + gval_b
        gcs = (gc_f, gc_b)
        gcts = (gc_f.T, gc_b.T)
        bvals = (bval_f, bval_b)
        gc_col, gc_row, beta, g_last, q_b, k_b, q, k, v, kbeta = ([] for _ in range(10))
        for (hh, d), rw in zip(units, rows):
            lg = d * DN_HEADS + head0 + hh
            gc_col.append(pick_col(gcs[d], lg))
            gc_row.append(pick_row(gcts[d], lg))
            beta.append(pick_col(bvals[d], 2 * DN_HEADS + lg))
            g_last.append(gc_col[-1][r - 1:r, :] if d == 0 else gc_col[-1][0:1, :])
            q_b.append(q_ref[rw, lanes[hh]])
            k_b.append(k_ref[rw, lanes[hh]])
            q.append(q_b[-1].astype(F32))
            k.append(k_b[-1].astype(F32))
            v.append(v_ref[rw, lanes[hh]].astype(F32))
            kbeta.append(k[-1] * beta[-1])
        qkk = [lax.dot_general(jnp.concatenate([q_b[u], kbeta[u].astype(BF16)], axis=0), k_b[u],
                               (((1,), (1,)), ((), ())), preferred_element_type=F32) for u in range(nu)]
        yield
        dec = [jnp.exp(gc_col[u] - gc_row[u] + nmask_sc[units[u][1]]) for u in range(nu)]
        attn = [(qkk[u][:r] * dec[u]).astype(BF16) for u in range(nu)]
        a_b = [(qkk[u][r:] * dec[u]).astype(BF16) for u in range(nu)]
        e_gc = [jnp.exp(gc_col[u]) for u in range(nu)]
        rhs = [jnp.concatenate([v[u] * beta[u], kbeta[u] * e_gc[u]], axis=1).astype(BF16) for u in range(nu)]
        qd = [(q[u] * e_gc[u]).astype(BF16) for u in range(nu)]
        kdt = [(k[u] * jnp.exp(g_last[u] - gc_col[u])).T.astype(BF16) for u in range(nu)]
        cd = [jnp.broadcast_to(jnp.exp(g), (8, hd)) for g in g_last]

        a_h = [a_b[u][halves[hf], halves[hf]] for u, hf in hidx]
        a_x = [a_b[u][halves[se], halves[fi]] for u, (fi, se) in enumerate(order)]
        y_f = [a * bmask_sc[0] for a in a_h]
        y_s = [strip(yy, DN_BASE) for yy in y_f]
        t_s = [ys.astype(F32) + pair_sc[len(DN_MERGE_SIZES), 0:DN_BASE, :] for ys in y_s]
        y_s = [_bdot(ys, yf).astype(BF16) for ys, yf in zip(y_s, y_f)]
        yield
        y_f = [expand(ys, 0) for ys in y_s]
        for _ in range(DN_BASE.bit_length() - 3):
            st = [_bdot(jnp.concatenate([ts.astype(BF16), ys], axis=0), yf) for ts, ys, yf in zip(t_s, y_s, y_f)]
            yield
            t_s = [ts + x[:DN_BASE] for ts, x in zip(t_s, st)]
            y_s = [x[DN_BASE:].astype(BF16) for x in st]
            y_f = [expand(ys, 0) for ys in y_s]
        t_s = [ts + _bdot(ts.astype(BF16), yf) for ts, yf in zip(t_s, y_f)]
        yield
        t_s = yield from merge_strips(t_s, a_h, 0)
        return tuple(t_s), tuple(a_h), tuple(a_x), tuple(attn), tuple(rhs), tuple(qd), tuple(kdt), tuple(cd)

    def merge_strips(t_s, a_h, lvl):
        s = DN_MERGE_SIZES[lvl]
        t_f = [expand(ts.astype(BF16), lvl) for ts in t_s]
        l_s = [strip(a * bmask_sc[1 + lvl], 2 * s) for a in a_h]
        m1 = [_bdot(ls, tf).astype(BF16) for ls, tf in zip(l_s, t_f)]
        yield
        m1_f = [expand(mm, lvl + 1) for mm in m1]
        t_s = [widen(ts, lvl) for ts in t_s]
        t_s = [ts - _bdot(ts.astype(BF16), mf) for ts, mf in zip(t_s, m1_f)]
        yield
        return t_s

    def stage_b(ctx):
        t_s, a_h, a_x, attn, rhs, qd, kdt, cd = ctx
        t_s = list(t_s)
        for lvl in range(1, top):
            t_s = yield from merge_strips(t_s, a_h, lvl)
        t = [widen(ts, top) for ts in t_s]
        t_b = [tt.astype(BF16) for tt in t]
        m1 = [_bdot(a * bmask_sc[1 + top], tb).astype(BF16) for a, tb in zip(a_h, t_b)]
        yield
        t = [tt - _bdot(tb, mm) for tt, tb, mm in zip(t, t_b, m1)]
        yield
        t_b = [tt.astype(BF16) for tt in t]
        p = [_bdot(a_x[u], t_b[2 * u + fi]).astype(BF16) for u, (fi, se) in enumerate(order)]
        yield
        x_b = [(-_bdot(t_b[2 * u + se], p[u])).astype(BF16) for u, (fi, se) in enumerate(order)]
        yield
        zero = jnp.zeros((h2, h2), BF16)
        t_full = []
        for u, (_, d) in enumerate(units):
            if d == 0:
                upper = jnp.concatenate([t_b[2 * u], zero], axis=1)
                lower = jnp.concatenate([x_b[u], t_b[2 * u + 1]], axis=1)
            else:
                upper = jnp.concatenate([t_b[2 * u], x_b[u]], axis=1)
                lower = jnp.concatenate([zero, t_b[2 * u + 1]], axis=1)
            t_full.append(jnp.concatenate([upper, lower], axis=0))
        uw = [_bdot(t_full[u], rhs[u]) for u in range(nu)]
        yield
        u_f = tuple(x[:, :hd] for x in uw)
        wq = tuple(jnp.concatenate([uw[u][:, hd:].astype(BF16), qd[u]], axis=0) for u in range(nu))
        return u_f, wq, attn, kdt, cd

    def stage_c(ctx, states, n):
        u_f, wq, attn, kdt, cd = ctx
        _, rows = block_rows(n)
        r1 = [_bdot(wq[u], states[u].astype(BF16)) for u in range(nu)]
        yield
        v_new = [(u_f[u] - r1[u][:r]).astype(BF16) for u in range(nu)]
        r2 = [_bdot(jnp.concatenate([attn[u], kdt[u]], axis=0), v_new[u]) for u in range(nu)]
        yield
        for u, (hh, _) in enumerate(units):
            oacc_sc[rows[u], lanes[hh]] += r1[u][r:] + r2[u][:r]
        return tuple(states[u] * cd[u][0:1, :] + r2[u][r:] for u in range(nu))

    def interleave(gens):
        results = [None] * len(gens)
        live = list(range(len(gens)))
        while live:
            for i in list(live):
                try:
                    next(gens[i])
                except StopIteration as stop:
                    results[i] = stop.value
                    live.remove(i)
        return results

    def scan(i, states):
        ns = [i * DN_BLOCKS_PER_ITER + b for b in range(DN_BLOCKS_PER_ITER)]
        ctx = interleave([stage_a(n) for n in ns])
        ctx = interleave([stage_b(c) for c in ctx])
        for c, n in zip(ctx, ns):
            (states,) = interleave([stage_c(c, states, n)])
        return states

    lax.fori_loop(0, n_blocks // DN_BLOCKS_PER_ITER, scan, (jnp.zeros((hd, hd), F32),) * nu)

    def finish(m, carry):
        rows = pl.ds(pl.multiple_of(m * r, r), r)
        for ln in lanes:
            o = oacc_sc[rows, ln]
            ms = jnp.mean(o * o, axis=-1, keepdims=True)
            o = o * lax.rsqrt(ms + RMS_EPS) * ong_ref[...]
            z = z_ref[rows, ln].astype(F32)
            o_ref[rows, ln] = (o * (z * jax.nn.sigmoid(z))).astype(o_ref.dtype)
        return carry

    lax.fori_loop(0, n_blocks, finish, 0)


def _dn_mixer(proj, tail, conv_w_stack, layer, a_log, dt_bias, out_g, batch, seq):
    r = DN_ROWS
    hb = DN_HEADS_PER_STEP
    n_blocks = seq // r
    n_groups = DN_HEADS // hb
    width = hb * HEAD_DIM
    proj3 = proj.reshape(batch, seq, proj.shape[1])
    tail3 = tail.reshape(batch, seq, DN_TAIL_W)
    pad = HEAD_DIM - 2 * DN_HEADS
    alog = jnp.pad(a_log.reshape(1, 2 * DN_HEADS).astype(F32), ((0, 0), (0, pad)))
    dtb = jnp.pad(dt_bias.reshape(1, 2 * DN_HEADS).astype(F32), ((0, 0), (0, pad)))
    q_n, k_n, v_n = _dn_prep(proj3, conv_w_stack, layer, batch, seq)
    col = lambda off: pl.BlockSpec((None, seq, width), lambda b, g: (b, 0, off * n_groups + g))
    row = pl.BlockSpec((1, HEAD_DIM), lambda b, g: (0, 0))
    out = pl.pallas_call(
        functools.partial(_dn_kernel, n_blocks=n_blocks, hb=hb),
        grid=(batch, n_groups),
        in_specs=[
            col(0), col(0), col(0), col(3),
            pl.BlockSpec((None, seq, HEAD_DIM), lambda b, g: (b, 0, MEM_W // HEAD_DIM),
                         pipeline_mode=pl.Buffered(1)),
            row, row, row,
        ],
        out_specs=pl.BlockSpec((None, seq, width), lambda b, g: (b, 0, g)),
        out_shape=jax.ShapeDtypeStruct((batch, seq, MIX_W), BF16),
        scratch_shapes=[
            pltpu.VMEM((seq, width), F32),
            pltpu.VMEM((2, r, r), F32),
            pltpu.VMEM((r, r), BF16),
            pltpu.VMEM((1 + len(DN_MERGE_SIZES), r // 2, r // 2), BF16),
            pltpu.VMEM((len(DN_MERGE_SIZES), r // 2, r // 2), BF16),
            pltpu.VMEM((1 + len(DN_MERGE_SIZES), r // 2, r // 2), F32),
        ],
        compiler_params=_cparams(("parallel", "arbitrary")),
        name="dn_mixer",
    )(q_n, k_n, v_n, proj3, tail3, alog, dtb, out_g.reshape(1, HEAD_DIM).astype(F32))
    return out.reshape(batch * seq, MIX_W)


def kernel(x, mem, mem_norm_g, tok_norm_g, dn_w_in, dn_conv_w, dn_a_log, dn_dt_bias, dn_out_norm_g,
           gm_w_in, gm_v_norm_g, gm_w_s, gm_b_s, mem_w_kv, tok_w_out, ffn_norm_g, ffn_w1, ffn_w2,
           final_norm_g):
    batch, seq, d = x.shape
    t = batch * seq
    h = x.reshape(t, d)
    mem2 = mem.reshape(batch * N_MEM, d)
    dn_w_t = jnp.swapaxes(dn_w_in, 1, 2)
    dn_w_tail = jnp.concatenate(
        [dn_w_t[:, 4 * MIX_W + DN_GATES:], dn_w_t[:, 4 * MIX_W:4 * MIX_W + DN_GATES],
         jnp.zeros((dn_w_t.shape[0], HEAD_DIM - DN_GATES, d), F32)], axis=1)
    for layer in range(DEPTH):
        j = layer // 2
        g_tok = tok_norm_g[layer]
        if layer % 2 == 0:
            proj, tail = _norm_matmul(h, g_tok, dn_w_t, j, 4 * MIX_W, 1024, 768, dn_w_tail, w_transposed=True)
            mix = _dn_mixer(proj, tail, dn_conv_w, j, dn_a_log[j], dn_dt_bias[j], dn_out_norm_g[j],
                            batch, seq)
            q_src, q_block = tail, 0
        else:
            proj = _norm_matmul(h, g_tok, gm_w_in, j, 2 * MIX_W + MEM_W, 1024, 896)
            mix = _gm_mixer(proj, gm_v_norm_g[j], gm_w_s, j, gm_b_s[j].T.astype(F32), 512)
            q_src, q_block = proj, 2 * MIX_W // MEM_W
        kv = _norm_matmul(mem2, mem_norm_g, mem_w_kv, layer, 2 * MEM_W, batch * N_MEM, 512)
        mo = _mem_attn(q_src, kv, batch, seq, q_block, min(1024, seq))
        h = _out_proj(h, mix, mo, tok_w_out, layer, 512)
        h = _ffn(h, ffn_norm_g[layer], ffn_w1, ffn_w2, layer, final_norm_g, layer == DEPTH - 1, 1024, 512)
    return h.reshape(batch, seq, d)
```
